```python
import math
import jax, jax.numpy as jnp
from jax import lax
import numpy as np

D_MODEL = 2048
BATCH = 2
SEQ = 4096
DEPTH = 1
DEC_BATCH = 128
DEC_SEQ = 8
PAST_LEN = 8192
PAGE_SIZE = 128

HEAD_DIM = 64
N_Q_HEADS = 16
N_KV_HEADS = 4
GQA = N_Q_HEADS // N_KV_HEADS
ATTN_WIDTH = N_Q_HEADS * HEAD_DIM
KV_WIDTH = N_KV_HEADS * HEAD_DIM
WINDOW = 128
ATTN_BLOCK = 128
ROPE_THETA = 10000.0
LRU_WIDTH = D_MODEL - ATTN_WIDTH
LRU_BLOCKS = 16
LRU_BLOCK = LRU_WIDTH // LRU_BLOCKS
CONV_W = 4
LRU_C = 8.0
IN_WIDTH = ATTN_WIDTH + 2 * KV_WIDTH + 2 * LRU_WIDTH
N_KEYS = 128
N_EXPERTS = N_KEYS * N_KEYS
PEER_HEADS = 8
PEER_TOPK = 16
D_KEY = 256
PEER_CHUNK = 128
EPS = 1e-6
NEG_INF = -1e30

kernel_name = "hymba_swa_sink_rglru_peer_step"


def _rmsnorm(x, g):
    xf = x.astype(jnp.float32)
    y = xf * lax.rsqrt(jnp.mean(xf * xf, axis=-1, keepdims=True) + EPS)
    return (y * g.astype(jnp.float32)).astype(x.dtype)


def _rope(x, pos):
    half = HEAD_DIM // 2
    inv = jnp.exp(-math.log(ROPE_THETA) * jnp.arange(half, dtype=jnp.float32) * (2.0 / HEAD_DIM))
    ang = pos[:, None] * inv[None, :]
    cos = jnp.cos(ang)[:, None, :]
    sin = jnp.sin(ang)[:, None, :]
    xf = x.astype(jnp.float32)
    x1, x2 = xf[..., :half], xf[..., half:]
    return jnp.concatenate([x1 * cos - x2 * sin, x2 * cos + x1 * sin], axis=-1).astype(x.dtype)


def _split_in(z):
    cuts = np.cumsum([ATTN_WIDTH, KV_WIDTH, KV_WIDTH, LRU_WIDTH]).tolist()
    return jnp.split(z, cuts, axis=-1)


def _qk_prep(q, k, v, pos, g_q, g_k):
    b, t = q.shape[:2]
    q = _rope(_rmsnorm(q.reshape(b, t, N_Q_HEADS, HEAD_DIM), g_q), pos)
    k = _rope(_rmsnorm(k.reshape(b, t, N_KV_HEADS, HEAD_DIM), g_k), pos)
    return q, k, v.reshape(b, t, N_KV_HEADS, HEAD_DIM)


def _sink_softmax(logits, mask, sink):
    logits = jnp.where(mask, logits, NEG_INF)
    m = jnp.maximum(jnp.max(logits, axis=-1, keepdims=True), sink)
    e = jnp.exp(logits - m)
    return e / (jnp.sum(e, axis=-1, keepdims=True) + jnp.exp(sink - m))


def _attn_banded(q, k, v, sink):
    b, t = q.shape[:2]
    nb = t // ATTN_BLOCK
    qb = q.reshape(b, nb, ATTN_BLOCK, N_KV_HEADS, GQA, HEAD_DIM)
    kb = k.reshape(b, nb, ATTN_BLOCK, N_KV_HEADS, HEAD_DIM)
    vb = v.reshape(b, nb, ATTN_BLOCK, N_KV_HEADS, HEAD_DIM)
    shift = lambda a: jnp.concatenate([jnp.zeros_like(a[:, :1]), a[:, :-1]], axis=1)
    kc = jnp.concatenate([shift(kb), kb], axis=2)
    vc = jnp.concatenate([shift(vb), vb], axis=2)
    logits = jnp.einsum('bnqkgd,bnskd->bnkgqs', qb, kc).astype(jnp.float32) * (HEAD_DIM ** -0.5)
    i = jnp.arange(ATTN_BLOCK)[:, None]
    j = jnp.arange(2 * ATTN_BLOCK)[None, :]
    diff = i + ATTN_BLOCK - j
    band = (diff >= 0) & (diff < WINDOW)
    mask = band[None] & ((jnp.arange(nb)[:, None, None] > 0) | (j[None] >= ATTN_BLOCK))
    p = _sink_softmax(logits, mask[:, None, None], sink)
    o = jnp.einsum('bnkgqs,bnskd->bnqkgd', p.astype(v.dtype), vc)
    return o.reshape(b, t, ATTN_WIDTH)


def _attn_window_cache(q, k, v, ck, cv, sink):
    b, s = q.shape[:2]
    w = ck.shape[1]
    kall = jnp.concatenate([ck.astype(k.dtype), k], axis=1)
    vall = jnp.concatenate([cv.astype(v.dtype), v], axis=1)
    qpos = PAST_LEN + jnp.arange(s)
    kpos = PAST_LEN - w + jnp.arange(w + s)
    diff = qpos[:, None] - kpos[None, :]
    mask = (diff >= 0) & (diff < WINDOW)
    qg = q.reshape(b, s, N_KV_HEADS, GQA, HEAD_DIM)
    logits = jnp.einsum('bqkgd,bskd->bkgqs', qg, kall).astype(jnp.float32) * (HEAD_DIM ** -0.5)
    p = _sink_softmax(logits, mask, sink)
    o = jnp.einsum('bkgqs,bskd->bqkgd', p.astype(v.dtype), vall)
    return o.reshape(b, s, ATTN_WIDTH), kall[:, -w:], vall[:, -w:]


def _rg_lru(xr, yg, conv_prev, h0, conv_w, conv_b, w_a, b_a, w_x, b_x, lam):
    b, t, _ = xr.shape
    xp = jnp.concatenate([conv_prev.astype(xr.dtype), xr], axis=1)
    xc = sum(xp[:, j:j + t] * conv_w[j] for j in range(CONV_W)) + conv_b
    xb = xc.reshape(b, t, LRU_BLOCKS, LRU_BLOCK)
    r = jax.nn.sigmoid(jnp.einsum('btnd,nde->btne', xb, w_a) + b_a).reshape(b, t, LRU_WIDTH)
    ig = jax.nn.sigmoid(jnp.einsum('btnd,nde->btne', xb, w_x) + b_x).reshape(b, t, LRU_WIDTH)
    log_a = -LRU_C * r.astype(jnp.float32) * jax.nn.softplus(-lam.astype(jnp.float32))
    a = jnp.exp(log_a)
    u = jnp.sqrt(-jnp.expm1(2.0 * log_a)) * (ig * xc).astype(jnp.float32)
    u = u.at[:, 0].add(a[:, 0] * h0.astype(jnp.float32))
    comb = lambda l, rr: (l[0] * rr[0], rr[0] * l[1] + rr[1])
    _, h = lax.associative_scan(comb, (a, u), axis=1)
    y = h.astype(xr.dtype) * jax.nn.gelu(yg)
    return y, xp[:, -(CONV_W - 1):], h[:, -1].astype(xr.dtype)


def _peer(h, w_pq, sub_keys, peer_u, peer_v):
    n = h.shape[0]
    pad = (-n) % PEER_CHUNK
    hb = jnp.pad(h, ((0, pad), (0, 0))).reshape(-1, PEER_CHUNK, D_MODEL)

    def one(hc):
        c = hc.shape[0]
        q = (hc @ w_pq).reshape(c, PEER_HEADS, 2, D_KEY // 2)
        s = jnp.einsum('nhcd,hckd->nhck', q, sub_keys).astype(jnp.float32)
        sv, si = lax.top_k(s, PEER_TOPK)
        cand = (sv[:, :, 0, :, None] + sv[:, :, 1, None, :]).reshape(c, PEER_HEADS, -1)
        cid = (si[:, :, 0, :, None] * N_KEYS + si[:, :, 1, None, :]).reshape(c, PEER_HEADS, -1)
        best, pos = lax.top_k(cand, PEER_TOPK)
        eid = jnp.take_along_axis(cid, pos, axis=-1)
        g = jax.nn.softmax(best, axis=-1)
        act = jax.nn.gelu(jnp.einsum('nd,nhkd->nhk', hc, peer_u[eid]).astype(jnp.float32))
        return jnp.einsum('nhk,nhkd->nd', (g * act).astype(hc.dtype), peer_v[eid])

    return lax.map(one, hb).reshape(-1, D_MODEL)[:n]


def _layer(x, pos0, kv_cache, conv_prev, h0, lw):
    (g_mix, w_in, g_q, g_k, sinks, conv_w, conv_b, w_a, b_a, w_x, b_x, lam,
     g_ao, g_lo, w_out, g_ffn, w_pq, sub_keys, peer_u, peer_v) = lw
    b, t, _ = x.shape
    hn = _rmsnorm(x, g_mix)
    q, k, v, xr, yg = _split_in(hn @ w_in)
    pos = pos0 + jnp.arange(t, dtype=jnp.float32)
    q, k, v = _qk_prep(q, k, v, pos, g_q, g_k)
    sink = sinks.astype(jnp.float32).reshape(N_KV_HEADS, GQA, 1, 1)
    if kv_cache is None:
        ao = _attn_banded(q, k, v, sink)
        k_rows, v_rows = k[:, -WINDOW:], v[:, -WINDOW:]
    else:
        ao, k_rows, v_rows = _attn_window_cache(q, k, v, kv_cache[0], kv_cache[1], sink)
    lo, conv_new, h_new = _rg_lru(xr, yg, conv_prev, h0, conv_w, conv_b, w_a, b_a, w_x, b_x, lam)
    mixed = jnp.concatenate([_rmsnorm(ao, g_ao), _rmsnorm(lo, g_lo)], axis=-1) @ w_out
    x = x + mixed
    f = _peer(_rmsnorm(x, g_ffn).reshape(b * t, D_MODEL), w_pq, sub_keys, peer_u, peer_v)
    x = x + f.reshape(b, t, D_MODEL)
    return x, k_rows, v_rows, conv_new, h_new


def setup_inputs(seed: int = 0) -> dict:
    key = jax.random.key(seed)
    ks = jax.random.split(key, 32)
    nrm = lambda k, shape, s: jax.random.normal(k, shape, jnp.float32) * s
    gain = lambda k, shape: 1.0 + 0.02 * jax.random.normal(k, shape, jnp.float32)
    cache_len = min(WINDOW, PAST_LEN)
    u = jax.random.uniform(ks[12], (DEPTH, LRU_WIDTH), jnp.float32, 0.9, 0.999)
    a0 = u ** (1.0 / LRU_C)
    return {
        "x_prompt": nrm(ks[0], (BATCH, SEQ, D_MODEL), 1.0),
        "x_sample": nrm(ks[1], (DEC_BATCH, DEC_SEQ, D_MODEL), 1.0),
        "cache_k": nrm(ks[2], (DEPTH, DEC_BATCH, cache_len, N_KV_HEADS, HEAD_DIM), 1.0),
        "cache_v": nrm(ks[3], (DEPTH, DEC_BATCH, cache_len, N_KV_HEADS, HEAD_DIM), 1.0),
        "state_conv": nrm(ks[4], (DEPTH, DEC_BATCH, CONV_W - 1, LRU_WIDTH), 1.0),
        "state_h": nrm(ks[5], (DEPTH, DEC_BATCH, LRU_WIDTH), 0.5),
        "norm_mix_g": gain(ks[6], (DEPTH, D_MODEL)),
        "w_in": nrm(ks[7], (DEPTH, D_MODEL, IN_WIDTH), D_MODEL ** -0.5),
        "q_norm_g": gain(ks[8], (DEPTH, HEAD_DIM)),
        "k_norm_g": gain(ks[9], (DEPTH, HEAD_DIM)),
        "attn_sinks": nrm(ks[10], (DEPTH, N_Q_HEADS), 1.0),
        "conv_w": nrm(ks[11], (DEPTH, CONV_W, LRU_WIDTH), CONV_W ** -0.5),
        "conv_b": nrm(ks[13], (DEPTH, LRU_WIDTH), 0.01),
        "w_rec_gate": nrm(ks[14], (DEPTH, LRU_BLOCKS, LRU_BLOCK, LRU_BLOCK), LRU_BLOCK ** -0.5),
        "b_rec_gate": nrm(ks[15], (DEPTH, LRU_BLOCKS, LRU_BLOCK), 0.01),
        "w_in_gate": nrm(ks[16], (DEPTH, LRU_BLOCKS, LRU_BLOCK, LRU_BLOCK), LRU_BLOCK ** -0.5),
        "b_in_gate": nrm(ks[17], (DEPTH, LRU_BLOCKS, LRU_BLOCK), 0.01),
        "lru_lambda": jnp.log(a0) - jnp.log1p(-a0),
        "attn_out_g": gain(ks[18], (DEPTH, ATTN_WIDTH)),
        "lru_out_g": gain(ks[19], (DEPTH, LRU_WIDTH)),
        "w_out": nrm(ks[20], (DEPTH, D_MODEL, D_MODEL), D_MODEL ** -0.5),
        "norm_ffn_g": gain(ks[21], (DEPTH, D_MODEL)),
        "w_peer_q": nrm(ks[22], (DEPTH, D_MODEL, PEER_HEADS * D_KEY), D_MODEL ** -0.5),
        "peer_sub_keys": nrm(ks[23], (DEPTH, PEER_HEADS, 2, N_KEYS, D_KEY // 2), (D_KEY // 2) ** -0.5),
        "peer_u": nrm(ks[24], (DEPTH, N_EXPERTS, D_MODEL), D_MODEL ** -0.5),
        "peer_v": nrm(ks[25], (DEPTH, N_EXPERTS, D_MODEL), PEER_TOPK ** -0.5),
    }


def reference(x_prompt, x_sample, cache_k, cache_v, state_conv, state_h,
              norm_mix_g, w_in, q_norm_g, k_norm_g, attn_sinks, conv_w, conv_b,
              w_rec_gate, b_rec_gate, w_in_gate, b_in_gate, lru_lambda,
              attn_out_g, lru_out_g, w_out, norm_ffn_g, w_peer_q, peer_sub_keys,
              peer_u, peer_v):
    xp, xs = x_prompt, x_sample
    kp, vp, cp, hp, kss, vss, css, hss = [], [], [], [], [], [], [], []
    bp = xp.shape[0]
    for l in range(DEPTH):
        lw = (norm_mix_g[l], w_in[l], q_norm_g[l], k_norm_g[l], attn_sinks[l], conv_w[l], conv_b[l],
              w_rec_gate[l], b_rec_gate[l], w_in_gate[l], b_in_gate[l], lru_lambda[l],
              attn_out_g[l], lru_out_g[l], w_out[l], norm_ffn_g[l], w_peer_q[l], peer_sub_keys[l],
              peer_u[l], peer_v[l])
        xp, k1, v1, c1, h1 = _layer(
            xp, 0.0, None,
            jnp.zeros((bp, CONV_W - 1, LRU_WIDTH), xp.dtype),
            jnp.zeros((bp, LRU_WIDTH), xp.dtype), lw)
        xs, k2, v2, c2, h2 = _layer(
            xs, float(PAST_LEN), (cache_k[l], cache_v[l]), state_conv[l], state_h[l], lw)
        kp.append(k1); vp.append(v1); cp.append(c1); hp.append(h1)
        kss.append(k2); vss.append(v2); css.append(c2); hss.append(h2)
    return (xp, xs,
            jnp.stack(kp), jnp.stack(vp), jnp.stack(cp), jnp.stack(hp),
            jnp.stack(kss), jnp.stack(vss), jnp.stack(css), jnp.stack(hss))
```

```python
import functools
import math

import jax
import jax.numpy as jnp
import numpy as np
from jax import lax
from jax.experimental import pallas as pl
from jax.experimental.pallas import tpu as pltpu

F32 = jnp.float32
BF16 = jnp.bfloat16

D_MODEL = 2048
HEAD_DIM = 64
N_Q_HEADS = 16
N_KV_HEADS = 4
GQA = N_Q_HEADS // N_KV_HEADS
ATTN_WIDTH = N_Q_HEADS * HEAD_DIM
KV_WIDTH = N_KV_HEADS * HEAD_DIM
WINDOW = 128
ROPE_THETA = 10000.0
LRU_WIDTH = D_MODEL - ATTN_WIDTH
LRU_BLOCK = 64
CONV_W = 4
LRU_C = 8.0
N_KEYS = 128
N_EXPERTS = N_KEYS * N_KEYS
PEER_HEADS = 8
PEER_TOPK = 16
D_KEY = 256
EPS = 1e-6
NEG_INF = -1e30

MXU_DIM = 256
LANES = 128
SUBLANES = 8
VMEM_LIMIT_BYTES = 60 * 1024 * 1024


def _cparams(sem):
    return pltpu.CompilerParams(dimension_semantics=sem, vmem_limit_bytes=VMEM_LIMIT_BYTES)


IN_CUTS = (0, ATTN_WIDTH, ATTN_WIDTH + KV_WIDTH, ATTN_WIDTH + 2 * KV_WIDTH,
           ATTN_WIDTH + 2 * KV_WIDTH + LRU_WIDTH, ATTN_WIDTH + 2 * KV_WIDTH + 2 * LRU_WIDTH)


def _inproj_body(x_ref, g_ref, w_ref, q_ref, k_ref, v_ref, xr_ref, yg_ref):
    x = x_ref[...]
    ms = jnp.mean(x * x, axis=-1, keepdims=True)
    hn = (x * lax.rsqrt(ms + EPS) * g_ref[...]).astype(BF16)
    for o_ref, lo, hi in zip((q_ref, k_ref, v_ref, xr_ref, yg_ref), IN_CUTS[:-1], IN_CUTS[1:]):
        o_ref[...] = jnp.dot(hn, w_ref[:, lo:hi], preferred_element_type=F32)


def _inproj(x, g, w_bf, tm):
    n = x.shape[0]
    widths = [hi - lo for lo, hi in zip(IN_CUTS[:-1], IN_CUTS[1:])]
    return pl.pallas_call(
        _inproj_body,
        grid=(n // tm,),
        in_specs=[
            pl.BlockSpec((tm, D_MODEL), lambda i: (i, 0)),
            pl.BlockSpec((1, D_MODEL), lambda i: (0, 0)),
            pl.BlockSpec((D_MODEL, IN_CUTS[-1]), lambda i: (0, 0)),
        ],
        out_specs=[pl.BlockSpec((tm, w), lambda i: (i, 0)) for w in widths],
        out_shape=[jax.ShapeDtypeStruct((n, w), F32) for w in widths],
        compiler_params=_cparams(("arbitrary",)),
        name="inproj",
    )(x, g, w_bf)


def _group_mean_sq(x, gmat):
    x2 = x * x
    hi = x2.astype(BF16)
    lo = (x2 - hi.astype(F32)).astype(BF16)
    cols = []
    for c in range(x.shape[1] // MXU_DIM):
        sl = slice(c * MXU_DIM, (c + 1) * MXU_DIM)
        cols.append(jnp.dot(hi[:, sl], gmat, preferred_element_type=F32)
                    + jnp.dot(lo[:, sl], gmat, preferred_element_type=F32))
    return cols[0] if len(cols) == 1 else jnp.concatenate(cols, axis=1)


def _tile_lanes(t, reps):
    return t if reps == 1 else jnp.concatenate([t] * reps, axis=1)


def _head_norm_rope(x, gain, cos2, sin2, gmat):
    width = x.shape[1]
    y = x * lax.rsqrt(_group_mean_sq(x, gmat) + EPS) * gain
    lane = lax.broadcasted_iota(jnp.int32, y.shape, 1)
    first_half = (lane & (HEAD_DIM // 2)) == 0
    partner = jnp.where(first_half,
                        pltpu.roll(y, width - HEAD_DIM // 2, axis=1),
                        pltpu.roll(y, HEAD_DIM // 2, axis=1))
    reps = width // LANES
    return y * _tile_lanes(cos2, reps) + partner * _tile_lanes(sin2, reps)


def _sink_softmax(lg, mask, sink):
    lg = jnp.where(mask, lg, NEG_INF)
    m = jnp.maximum(jnp.max(lg, axis=-1, keepdims=True), sink)
    e = jnp.exp(lg - m)
    return e / (jnp.sum(e, axis=-1, keepdims=True) + jnp.exp(sink - m))


def _rms_rows(x, gain):
    ms = jnp.mean(x * x, axis=-1, keepdims=True)
    return x * lax.rsqrt(ms + EPS) * gain


def _attn_prompt_body(q_ref, k_ref, v_ref, cos_ref, sin_ref, gq_ref, gk_ref, sink_ref, gao_ref, gmat_ref,
                      ao_ref, klast_ref, kprev, vprev, *, nb):
    n = pl.program_id(1)
    blk = WINDOW

    @pl.when(n == 0)
    def _():
        kprev[...] = jnp.zeros_like(kprev)
        vprev[...] = jnp.zeros_like(vprev)

    gmat = gmat_ref[...]
    cos2 = cos_ref[...]
    sin2 = sin_ref[...]
    q = _head_norm_rope(q_ref[...], gq_ref[...], cos2, sin2, gmat) * (HEAD_DIM ** -0.5)
    k = _head_norm_rope(k_ref[...], gk_ref[...], cos2, sin2, gmat)
    v = v_ref[...]
    kc = jnp.concatenate([kprev[...], k], axis=0).astype(BF16)
    vc = jnp.concatenate([vprev[...], v], axis=0).astype(BF16)
    qb = q.astype(BF16)

    rows = GQA * blk
    i = lax.broadcasted_iota(jnp.int32, (rows, 2 * blk), 0) & (blk - 1)
    j = lax.broadcasted_iota(jnp.int32, (rows, 2 * blk), 1)
    mask = (j > i) & (j <= i + blk) & ((n > 0) | (j >= blk))

    pieces = []
    for g in range(N_KV_HEADS):
        qg = jnp.concatenate(
            [qb[:, (g * GQA + h) * HEAD_DIM:(g * GQA + h + 1) * HEAD_DIM] for h in range(GQA)], axis=0)
        kg = kc[:, g * HEAD_DIM:(g + 1) * HEAD_DIM]
        vg = vc[:, g * HEAD_DIM:(g + 1) * HEAD_DIM]
        lg = lax.dot_general(qg, kg, (((1,), (1,)), ((), ())), preferred_element_type=F32)
        p = _sink_softmax(lg, mask, sink_ref[g])
        og = jnp.dot(p.astype(BF16), vg, preferred_element_type=F32)
        pieces.extend(og[h * blk:(h + 1) * blk] for h in range(GQA))
    ao = jnp.concatenate(pieces, axis=1)
    ao_ref[...] = _rms_rows(ao, gao_ref[...]).astype(BF16)

    kprev[...] = k
    vprev[...] = v

    @pl.when(n == nb - 1)
    def _():
        klast_ref[0] = k


def _attn_prompt(q, k, v, cos2, sin2, gq, gk, sink_col, gao, gmat, batch, seq):
    nb = seq // WINDOW
    row = lambda b, n: (b * nb + n, 0)
    const2 = lambda b, n: (0, 0)
    return pl.pallas_call(
        functools.partial(_attn_prompt_body, nb=nb),
        grid=(batch, nb),
        in_specs=[
            pl.BlockSpec((WINDOW, ATTN_WIDTH), row),
            pl.BlockSpec((WINDOW, KV_WIDTH), row),
            pl.BlockSpec((WINDOW, KV_WIDTH), row),
            pl.BlockSpec((WINDOW, LANES), lambda b, n: (n, 0)),
            pl.BlockSpec((WINDOW, LANES), lambda b, n: (n, 0)),
            pl.BlockSpec((1, ATTN_WIDTH), const2),
            pl.BlockSpec((1, KV_WIDTH), const2),
            pl.BlockSpec((N_KV_HEADS, GQA * WINDOW, 1), lambda b, n: (0, 0, 0)),
            pl.BlockSpec((1, ATTN_WIDTH), const2),
            pl.BlockSpec((MXU_DIM, MXU_DIM), const2),
        ],
        out_specs=[
            pl.BlockSpec((WINDOW, ATTN_WIDTH), row),
            pl.BlockSpec((1, WINDOW, KV_WIDTH), lambda b, n: (b, 0, 0)),
        ],
        out_shape=[
            jax.ShapeDtypeStruct((batch * seq, ATTN_WIDTH), BF16),
            jax.ShapeDtypeStruct((batch, WINDOW, KV_WIDTH), F32),
        ],
        scratch_shapes=[pltpu.VMEM((WINDOW, KV_WIDTH), F32), pltpu.VMEM((WINDOW, KV_WIDTH), F32)],
        compiler_params=_cparams(("arbitrary", "arbitrary")),
        name="attn_prompt",
    )(q, k, v, cos2, sin2, gq, gk, sink_col, gao, gmat)


def _attn_sample_body(q_ref, k_ref, v_ref, ck_ref, cv_ref, cos_ref, sin_ref, gq_ref, gk_ref, sink_ref, gao_ref,
                      gmat_ref, ao_ref, ks_ref, vs_ref, *, bb, s):
    gmat = gmat_ref[...]
    cos2 = cos_ref[...]
    sin2 = sin_ref[...]
    q = _head_norm_rope(q_ref[...], gq_ref[...], cos2, sin2, gmat) * (HEAD_DIM ** -0.5)
    k = _head_norm_rope(k_ref[...], gk_ref[...], cos2, sin2, gmat)
    v = v_ref[...]
    ck = ck_ref[...]
    cv = cv_ref[...]
    k3 = k.reshape(bb, s, KV_WIDTH)
    v3 = v.reshape(bb, s, KV_WIDTH)
    ks_ref[:, :WINDOW - s, :] = ck[:, s:, :]
    ks_ref[:, WINDOW - s:, :] = k3
    vs_ref[:, :WINDOW - s, :] = cv[:, s:, :]
    vs_ref[:, WINDOW - s:, :] = v3

    q3 = q.astype(BF16).reshape(bb, s, ATTN_WIDTH)
    ckb = ck.astype(BF16)
    cvb = cv.astype(BF16)
    k3b = k3.astype(BF16)
    v3b = v3.astype(BF16)
    rows = GQA * s
    qi = lax.broadcasted_iota(jnp.int32, (bb, rows, WINDOW), 1) & (s - 1)
    tc = lax.broadcasted_iota(jnp.int32, (bb, rows, WINDOW), 2)
    mask_c = tc > qi
    qi2 = lax.broadcasted_iota(jnp.int32, (bb, rows, s), 1) & (s - 1)
    tn = lax.broadcasted_iota(jnp.int32, (bb, rows, s), 2)
    mask_n = tn <= qi2

    pieces = []
    for g in range(N_KV_HEADS):
        qg = jnp.concatenate(
            [q3[:, :, (g * GQA + h) * HEAD_DIM:(g * GQA + h + 1) * HEAD_DIM] for h in range(GQA)], axis=1)
        sl = slice(g * HEAD_DIM, (g + 1) * HEAD_DIM)
        lc = jnp.einsum("bqd,bkd->bqk", qg, ckb[:, :, sl], preferred_element_type=F32)
        ln = jnp.einsum("bqd,bkd->bqk", qg, k3b[:, :, sl], preferred_element_type=F32)
        lc = jnp.where(mask_c, lc, NEG_INF)
        ln = jnp.where(mask_n, ln, NEG_INF)
        sink = sink_ref[g]
        m = jnp.maximum(jnp.maximum(jnp.max(lc, axis=-1, keepdims=True), jnp.max(ln, axis=-1, keepdims=True)), sink)
        ec = jnp.exp(lc - m)
        en = jnp.exp(ln - m)
        den = jnp.sum(ec, axis=-1, keepdims=True) + jnp.sum(en, axis=-1, keepdims=True) + jnp.exp(sink - m)
        og = (jnp.einsum("bqk,bkd->bqd", (ec / den).astype(BF16), cvb[:, :, sl], preferred_element_type=F32)
              + jnp.einsum("bqk,bkd->bqd", (en / den).astype(BF16), v3b[:, :, sl], preferred_element_type=F32))
        pieces.extend(og[:, h * s:(h + 1) * s, :] for h in range(GQA))
    ao = jnp.concatenate(pieces, axis=2).reshape(bb * s, ATTN_WIDTH)
    ao_ref[...] = _rms_rows(ao, gao_ref[...]).astype(BF16)


def _attn_sample(q, k, v, ck, cv, cos2, sin2, gq, gk, sink_col, gao, gmat, batch, s, bb):
    rows = bb * s
    row = lambda i: (i, 0)
    const2 = lambda i: (0, 0)
    b3 = lambda i: (i, 0, 0)
    return pl.pallas_call(
        functools.partial(_attn_sample_body, bb=bb, s=s),
        grid=(batch // bb,),
        in_specs=[
            pl.BlockSpec((rows, ATTN_WIDTH), row),
            pl.BlockSpec((rows, KV_WIDTH), row),
            pl.BlockSpec((rows, KV_WIDTH), row),
            pl.BlockSpec((bb, WINDOW, KV_WIDTH), b3),
            pl.BlockSpec((bb, WINDOW, KV_WIDTH), b3),
            pl.BlockSpec((rows, LANES), const2),
            pl.BlockSpec((rows, LANES), const2),
            pl.BlockSpec((1, ATTN_WIDTH), const2),
            pl.BlockSpec((1, KV_WIDTH), const2),
            pl.BlockSpec((N_KV_HEADS, GQA * s, 1), lambda i: (0, 0, 0)),
            pl.BlockSpec((1, ATTN_WIDTH), const2),
            pl.BlockSpec((MXU_DIM, MXU_DIM), const2),
        ],
        out_specs=[
            pl.BlockSpec((rows, ATTN_WIDTH), row),
            pl.BlockSpec((bb, WINDOW, KV_WIDTH), b3),
            pl.BlockSpec((bb, WINDOW, KV_WIDTH), b3),
        ],
        out_shape=[
            jax.ShapeDtypeStruct((batch * s, ATTN_WIDTH), BF16),
            jax.ShapeDtypeStruct((batch, WINDOW, KV_WIDTH), F32),
            jax.ShapeDtypeStruct((batch, WINDOW, KV_WIDTH), F32),
        ],
        compiler_params=_cparams(("arbitrary",)),
        name="attn_sample",
    )(q, k, v, ck, cv, cos2, sin2, gq, gk, sink_col, gao, gmat)


def _lru_gates(xc, wa_ref, wx_ref, ba, bx, lam):
    xb = xc.astype(BF16)
    r_cols, i_cols = [], []
    for c in range(LRU_WIDTH // MXU_DIM):
        sl = slice(c * MXU_DIM, (c + 1) * MXU_DIM)
        r_cols.append(jnp.dot(xb[:, sl], wa_ref[c], preferred_element_type=F32))
        i_cols.append(jnp.dot(xb[:, sl], wx_ref[c], preferred_element_type=F32))
    r = 1.0 / (1.0 + jnp.exp(-(jnp.concatenate(r_cols, axis=1) + ba)))
    ig = 1.0 / (1.0 + jnp.exp(-(jnp.concatenate(i_cols, axis=1) + bx)))
    neg_lam = -lam
    softplus = jnp.maximum(neg_lam, 0.0) + jnp.log1p(jnp.exp(-jnp.abs(neg_lam)))
    log_a = -LRU_C * r * softplus
    a = jnp.exp(log_a)
    u = jnp.sqrt(-jnp.tanh(log_a) * (a * a + 1.0)) * (ig * xc)
    return a, u


def _segment_scan(a, u, seg):
    t = lax.broadcasted_iota(jnp.int32, a.shape, 0) & (seg - 1)
    d = 1
    while d < seg:
        valid = t >= d
        u = jnp.where(valid, a * pltpu.roll(u, d, axis=0) + u, u)
        a = jnp.where(valid, a * pltpu.roll(a, d, axis=0), a)
        d *= 2
    return a, u


def _lru_prompt_body(xr_ref, yg_ref, cw_ref, cb_ref, wa_ref, wx_ref, ba_ref, bx_ref, lam_ref, glo_ref,
                     lo_ref, hlast_ref, prev8, hcarry, *, tb):
    t_blk = pl.program_id(1)

    @pl.when(t_blk == 0)
    def _():
        prev8[...] = jnp.zeros_like(prev8)
        hcarry[...] = jnp.zeros_like(hcarry)

    xr = xr_ref[...]
    p8 = prev8[...]
    row8 = lax.broadcasted_iota(jnp.int32, p8.shape, 0)
    xc = xr * cw_ref[CONV_W - 1:CONV_W, :] + cb_ref[...]
    for kshift in range(1, CONV_W):
        rolled = pltpu.roll(xr, kshift, axis=0)
        top = jnp.where(row8 < kshift, pltpu.roll(p8, kshift, axis=0), rolled[:SUBLANES])
        shifted = jnp.concatenate([top, rolled[SUBLANES:]], axis=0)
        xc = xc + shifted * cw_ref[CONV_W - 1 - kshift:CONV_W - kshift, :]
    a, u = _lru_gates(xc, wa_ref, wx_ref, ba_ref[...], bx_ref[...], lam_ref[...])
    acum, hloc = _segment_scan(a, u, tb)
    h = hloc + acum * hcarry[...]
    y = h * jax.nn.gelu(yg_ref[...])
    lo_ref[...] = _rms_rows(y, glo_ref[...]).astype(BF16)
    hlast = h[tb - 1:tb, :]
    hcarry[...] = hlast
    prev8[...] = xr[tb - SUBLANES:, :]
    hlast_ref[0] = hlast


def _lru_prompt(xr, yg, cw, cb, wa_bd, wx_bd, ba, bx, lam, glo, batch, seq, tb):
    nt = seq // tb
    row = lambda b, t: (b * nt + t, 0)
    const2 = lambda b, t: (0, 0)
    const3 = lambda b, t: (0, 0, 0)
    nchunk = LRU_WIDTH // MXU_DIM
    return pl.pallas_call(
        functools.partial(_lru_prompt_body, tb=tb),
        grid=(batch, nt),
        in_specs=[
            pl.BlockSpec((tb, LRU_WIDTH), row),
            pl.BlockSpec((tb, LRU_WIDTH), row),
            pl.BlockSpec((CONV_W, LRU_WIDTH), const2),
            pl.BlockSpec((1, LRU_WIDTH), const2),
            pl.BlockSpec((nchunk, MXU_DIM, MXU_DIM), const3),
            pl.BlockSpec((nchunk, MXU_DIM, MXU_DIM), const3),
            pl.BlockSpec((1, LRU_WIDTH), const2),
            pl.BlockSpec((1, LRU_WIDTH), const2),
            pl.BlockSpec((1, LRU_WIDTH), const2),
            pl.BlockSpec((1, LRU_WIDTH), const2),
        ],
        out_specs=[
            pl.BlockSpec((tb, LRU_WIDTH), row),
            pl.BlockSpec((1, 1, LRU_WIDTH), lambda b, t: (b, 0, 0)),
        ],
        out_shape=[
            jax.ShapeDtypeStruct((batch * seq, LRU_WIDTH), BF16),
            jax.ShapeDtypeStruct((batch, 1, LRU_WIDTH), F32),
        ],
        scratch_shapes=[pltpu.VMEM((SUBLANES, LRU_WIDTH), F32), pltpu.VMEM((1, LRU_WIDTH), F32)],
        compiler_params=_cparams(("arbitrary", "arbitrary")),
        name="lru_prompt",
    )(xr, yg, cw, cb, wa_bd, wx_bd, ba, bx, lam, glo)


def _lru_sample_body(xr_ref, yg_ref, cpad_ref, h0_ref, cw_ref, cb_ref, wa_ref, wx_ref, ba_ref, bx_ref, lam_ref,
                     glo_ref, lo_ref, h_ref, *, s):
    xr = xr_ref[...]
    cpad = cpad_ref[...]
    rows = xr.shape[0]
    t = lax.broadcasted_iota(jnp.int32, xr.shape, 0) & (s - 1)
    xc = xr * cw_ref[CONV_W - 1:CONV_W, :] + cb_ref[...]
    for kshift in range(1, CONV_W):
        shifted = jnp.where(t < kshift, pltpu.roll(cpad, rows - s + kshift, axis=0), pltpu.roll(xr, kshift, axis=0))
        xc = xc + shifted * cw_ref[CONV_W - 1 - kshift:CONV_W - kshift, :]
    a, u = _lru_gates(xc, wa_ref, wx_ref, ba_ref[...], bx_ref[...], lam_ref[...])
    acum, hloc = _segment_scan(a, u, s)
    h = hloc + acum * h0_ref[...]
    h_ref[...] = h
    y = h * jax.nn.gelu(yg_ref[...])
    lo_ref[...] = _rms_rows(y, glo_ref[...]).astype(BF16)


def _lru_sample(xr, yg, cpad, h0rows, cw, cb, wa_bd, wx_bd, ba, bx, lam, glo, s, tb):
    n = xr.shape[0]
    row = lambda i: (i, 0)
    const2 = lambda i: (0, 0)
    const3 = lambda i: (0, 0, 0)
    nchunk = LRU_WIDTH // MXU_DIM
    return pl.pallas_call(
        functools.partial(_lru_sample_body, s=s),
        grid=(n // tb,),
        in_specs=[
            pl.BlockSpec((tb, LRU_WIDTH), row),
            pl.BlockSpec((tb, LRU_WIDTH), row),
            pl.BlockSpec((tb, LRU_WIDTH), row),
            pl.BlockSpec((tb, LRU_WIDTH), row),
            pl.BlockSpec((CONV_W, LRU_WIDTH), const2),
            pl.BlockSpec((1, LRU_WIDTH), const2),
            pl.BlockSpec((nchunk, MXU_DIM, MXU_DIM), const3),
            pl.BlockSpec((nchunk, MXU_DIM, MXU_DIM), const3),
            pl.BlockSpec((1, LRU_WIDTH), const2),
            pl.BlockSpec((1, LRU_WIDTH), const2),
            pl.BlockSpec((1, LRU_WIDTH), const2),
            pl.BlockSpec((1, LRU_WIDTH), const2),
        ],
        out_specs=[pl.BlockSpec((tb, LRU_WIDTH), row), pl.BlockSpec((tb, LRU_WIDTH), row)],
        out_shape=[jax.ShapeDtypeStruct((n, LRU_WIDTH), BF16), jax.ShapeDtypeStruct((n, LRU_WIDTH), F32)],
        compiler_params=_cparams(("arbitrary",)),
        name="lru_sample",
    )(xr, yg, cpad, h0rows, cw, cb, wa_bd, wx_bd, ba, bx, lam, glo)


def _outproj_body(x_ref, ao_ref, lo_ref, w_ref, g_ref, x1_ref, hn_ref):
    mixed = (jnp.dot(ao_ref[...], w_ref[:ATTN_WIDTH, :], preferred_element_type=F32)
             + jnp.dot(lo_ref[...], w_ref[ATTN_WIDTH:, :], preferred_element_type=F32))
    x1 = x_ref[...] + mixed
    x1_ref[...] = x1
    hn_ref[...] = _rms_rows(x1, g_ref[...]).astype(BF16)


def _outproj(x, ao, lo, w_bf, g, tm):
    n = x.shape[0]
    row = lambda i: (i, 0)
    const2 = lambda i: (0, 0)
    return pl.pallas_call(
        _outproj_body,
        grid=(n // tm,),
        in_specs=[
            pl.BlockSpec((tm, D_MODEL), row),
            pl.BlockSpec((tm, ATTN_WIDTH), row),
            pl.BlockSpec((tm, LRU_WIDTH), row),
            pl.BlockSpec((D_MODEL, D_MODEL), const2),
            pl.BlockSpec((1, D_MODEL), const2),
        ],
        out_specs=[pl.BlockSpec((tm, D_MODEL), row), pl.BlockSpec((tm, D_MODEL), row)],
        out_shape=[jax.ShapeDtypeStruct((n, D_MODEL), F32), jax.ShapeDtypeStruct((n, D_MODEL), BF16)],
        compiler_params=_cparams(("arbitrary",)),
        name="outproj",
    )(x, ao, lo, w_bf, g)


PEER_CAND_COLS = tuple(min(PEER_TOPK, (PEER_TOPK + 1) // (a + 1)) for a in range(PEER_TOPK))
ROUTE_LANES = 128


def _extract_max(cur, rowf, big):
    m = jnp.max(cur, axis=0, keepdims=True)
    idx = jnp.min(jnp.where(cur == m, rowf, big), axis=0, keepdims=True)
    hit = rowf == idx
    return m, hit


def _top16(s):
    rowf = lax.broadcasted_iota(jnp.int32, s.shape, 0).astype(F32)
    cur = s
    kept = jnp.full_like(s, -jnp.inf)
    vals = []
    for _ in range(PEER_TOPK):
        m, hit = _extract_max(cur, rowf, float(N_KEYS))
        vals.append(m)
        kept = jnp.where(hit, m, kept)
        cur = jnp.where(hit, -jnp.inf, cur)
    return vals, kept


def _route_body(hn_ref, wq_ref, keys_ref, s1m_ref, e1z_ref, t0_ref, e0_ref, s_scr, *, tm):
    qt = lax.dot_general(wq_ref[...], hn_ref[...], (((1,), (1,)), ((), ())), preferred_element_type=F32)
    half = D_KEY // 2
    for hc in range(2 * PEER_HEADS):
        qhc = qt[hc * half:(hc + 1) * half, :].astype(BF16)
        s_scr[hc] = jnp.dot(keys_ref[hc], qhc, preferred_element_type=F32)

    def step(it, carry):
        h = it // (tm // ROUTE_LANES)
        c = it % (tm // ROUTE_LANES)
        lanes = pl.ds(pl.multiple_of(c * ROUTE_LANES, ROUTE_LANES), ROUTE_LANES)
        v0, s0m = _top16(s_scr[2 * h, :, lanes])
        v1, s1m = _top16(s_scr[2 * h + 1, :, lanes])
        v0_hi = jnp.concatenate(v0[SUBLANES:], axis=0)
        v1_all = jnp.concatenate(v1, axis=0)
        v1_lo = v1_all[:SUBLANES]
        row8 = lax.broadcasted_iota(jnp.int32, v1_lo.shape, 0)
        groups = [v1_all + v0[0]]
        for a in range(1, SUBLANES):
            groups.append(jnp.where(row8 < PEER_CAND_COLS[a], v1_lo + v0[a], -jnp.inf))
        groups.append(v0_hi + v1[0])
        cand = jnp.concatenate(groups, axis=0)
        rowf = lax.broadcasted_iota(jnp.int32, cand.shape, 0).astype(F32)
        best = []
        for _ in range(PEER_TOPK + 1):
            m, hit = _extract_max(cand, rowf, float(cand.shape[0]))
            best.append(m)
            cand = jnp.where(hit, -jnp.inf, cand)
        mx = best[0]
        z = jnp.exp(best[0] - mx)
        for r in range(1, PEER_TOPK):
            z = z + jnp.exp(best[r] - mx)
        thr = 0.5 * (best[PEER_TOPK - 1] + best[PEER_TOPK])
        s1m_ref[h, :, lanes] = s1m
        e1z_ref[h, :, lanes] = jnp.exp(s1m - v1[0]) / z
        t0_ref[h, :, lanes] = thr - s0m
        e0_ref[h, :, lanes] = jnp.exp(s0m - v0[0])
        return carry

    lax.fori_loop(0, PEER_HEADS * (tm // ROUTE_LANES), step, 0)


def _route(hn, wq_t, keys, tm):
    n = hn.shape[0]
    blk = lambda i: (0, 0, i)
    shape = jax.ShapeDtypeStruct((PEER_HEADS, N_KEYS, n), F32)
    spec = pl.BlockSpec((PEER_HEADS, N_KEYS, tm), blk)
    return pl.pallas_call(
        functools.partial(_route_body, tm=tm),
        grid=(n // tm,),
        in_specs=[
            pl.BlockSpec((tm, D_MODEL), lambda i: (i, 0)),
            pl.BlockSpec((PEER_HEADS * D_KEY, D_MODEL), lambda i: (0, 0)),
            pl.BlockSpec((2 * PEER_HEADS, N_KEYS, D_KEY // 2), lambda i: (0, 0, 0)),
        ],
        out_specs=[spec, spec, spec, spec],
        out_shape=[shape, shape, shape, shape],
        scratch_shapes=[pltpu.VMEM((2 * PEER_HEADS, N_KEYS, tm), F32)],
        compiler_params=_cparams(("arbitrary",)),
        name="peer_route",
    )(hn, wq_t, keys)


EXPERT_LANES = 256


def _experts_body(hn_ref, u_ref, vt_ref, s1m_ref, e1z_ref, t0_ref, e0_ref, x1_ref, y_ref, acc, act, pt, *, tm, eb, ne):
    e = pl.program_id(1)

    @pl.when(e == 0)
    def _():
        acc[...] = jnp.zeros_like(acc)

    act[...] = lax.dot_general(u_ref[...], hn_ref[...], (((1,), (1,)), ((), ())), preferred_element_type=F32)

    for ii in range(eb // N_KEYS):
        irow = e * (eb // N_KEYS) + ii
        rows = slice(ii * N_KEYS, (ii + 1) * N_KEYS)

        def chunk(c, carry, irow=irow, rows=rows):
            lanes = pl.ds(pl.multiple_of(c * EXPERT_LANES, EXPERT_LANES), EXPERT_LANES)
            w = jnp.zeros((N_KEYS, EXPERT_LANES), F32)
            for h in range(PEER_HEADS):
                trow = t0_ref[h, pl.ds(irow, 1), lanes]
                e0row = e0_ref[h, pl.ds(irow, 1), lanes]
                w = w + jnp.where(s1m_ref[h, :, lanes] >= trow, e1z_ref[h, :, lanes], 0.0) * e0row
            pt[rows, lanes] = (w * jax.nn.gelu(act[rows, lanes])).astype(BF16)
            return carry

        lax.fori_loop(0, tm // EXPERT_LANES, chunk, 0)

    acc[...] += jnp.dot(vt_ref[...], pt[...], preferred_element_type=F32)

    @pl.when(e == ne - 1)
    def _():
        y_ref[...] = x1_ref[...] + acc[...].T


def _experts(hn, u_bf, vt_bf, s1m, e1z, t0, e0, x1, tm, eb):
    n = hn.shape[0]
    ne = N_EXPERTS // eb
    tok = lambda t, e: (t, 0)
    small = pl.BlockSpec((PEER_HEADS, N_KEYS, tm), lambda t, e: (0, 0, t))
    return pl.pallas_call(
        functools.partial(_experts_body, tm=tm, eb=eb, ne=ne),
        grid=(n // tm, ne),
        in_specs=[
            pl.BlockSpec((tm, D_MODEL), tok),
            pl.BlockSpec((eb, D_MODEL), lambda t, e: (e, 0)),
            pl.BlockSpec((D_MODEL, eb), lambda t, e: (0, e)),
            small, small, small, small,
            pl.BlockSpec((tm, D_MODEL), tok),
        ],
        out_specs=pl.BlockSpec((tm, D_MODEL), tok),
        out_shape=jax.ShapeDtypeStruct((n, D_MODEL), F32),
        scratch_shapes=[
            pltpu.VMEM((D_MODEL, tm), F32),
            pltpu.VMEM((eb, tm), F32),
            pltpu.VMEM((eb, tm), BF16),
        ],
        compiler_params=_cparams(("arbitrary", "arbitrary")),
        name="peer_experts",
    )(hn, u_bf, vt_bf, s1m, e1z, t0, e0, x1)


def _rope_tables(pos):
    half = HEAD_DIM // 2
    inv = jnp.exp(-math.log(ROPE_THETA) * jnp.arange(half, dtype=F32) * (2.0 / HEAD_DIM))
    ang = pos[:, None] * inv[None, :]
    cos, sin = jnp.cos(ang), jnp.sin(ang)
    cos2 = jnp.concatenate([cos, cos, cos, cos], axis=1)
    sin2 = jnp.concatenate([-sin, sin, -sin, sin], axis=1)
    return cos2, sin2


def _block_diag(w):
    per = MXU_DIM // LRU_BLOCK
    w4 = w.reshape(LRU_WIDTH // MXU_DIM, per, LRU_BLOCK, LRU_BLOCK)
    eye = jnp.eye(per, dtype=w.dtype)
    return jnp.einsum("cpde,pq->cpdqe", w4, eye).reshape(LRU_WIDTH // MXU_DIM, MXU_DIM, MXU_DIM).astype(BF16)


def kernel(x_prompt, x_sample, cache_k, cache_v, state_conv, state_h, norm_mix_g, w_in, q_norm_g, k_norm_g, attn_sinks, conv_w, conv_b, w_rec_gate, b_rec_gate, w_in_gate, b_in_gate, lru_lambda, attn_out_g, lru_out_g, w_out, norm_ffn_g, w_peer_q, peer_sub_keys, peer_u, peer_v):
    depth = w_in.shape[0]
    assert depth == 1
    batch, seq, _ = x_prompt.shape
    dbatch, dseq, _ = x_sample.shape
    past_len = 8192
    l = 0

    w_in_bf = w_in[l].astype(BF16)
    w_out_bf = w_out[l].astype(BF16)
    wq_t = w_peer_q[l].T.astype(BF16)
    keys = peer_sub_keys[l].reshape(2 * PEER_HEADS, N_KEYS, D_KEY // 2).astype(BF16)
    u_bf = peer_u[l].astype(BF16)
    vt_bf = peer_v[l].T.astype(BF16)
    wa_bd = _block_diag(w_rec_gate[l])
    wx_bd = _block_diag(w_in_gate[l])
    ba = b_rec_gate[l].reshape(1, LRU_WIDTH)
    bx = b_in_gate[l].reshape(1, LRU_WIDTH)
    lam = lru_lambda[l].reshape(1, LRU_WIDTH)
    cw = conv_w[l]
    cb = conv_b[l].reshape(1, LRU_WIDTH)
    g_mix = norm_mix_g[l].reshape(1, D_MODEL)
    g_ffn = norm_ffn_g[l].reshape(1, D_MODEL)
    gq = jnp.tile(q_norm_g[l], N_Q_HEADS).reshape(1, ATTN_WIDTH)
    gk = jnp.tile(k_norm_g[l], N_KV_HEADS).reshape(1, KV_WIDTH)
    gao = attn_out_g[l].reshape(1, ATTN_WIDTH)
    glo = lru_out_g[l].reshape(1, LRU_WIDTH)
    gmat = jnp.asarray(np.kron(np.eye(MXU_DIM // HEAD_DIM), np.full((HEAD_DIM, HEAD_DIM), 1.0 / HEAD_DIM)), BF16)
    sinks = attn_sinks[l].astype(F32).reshape(N_KV_HEADS, GQA)
    sink_p = jnp.repeat(sinks, WINDOW, axis=1).reshape(N_KV_HEADS, GQA * WINDOW, 1)
    sink_s = jnp.repeat(sinks, dseq, axis=1).reshape(N_KV_HEADS, GQA * dseq, 1)
    cos_p, sin_p = _rope_tables(jnp.arange(seq, dtype=F32))
    cos_s, sin_s = _rope_tables(float(past_len) + jnp.arange(dseq, dtype=F32))

    n_p = batch * seq
    n_s = dbatch * dseq
    xp2 = x_prompt.reshape(n_p, D_MODEL)
    xs2 = x_sample.reshape(n_s, D_MODEL)

    qp, kp, vp, xrp, ygp = _inproj(xp2, g_mix, w_in_bf, 256)
    ao_p, k_last = _attn_prompt(qp, kp, vp, cos_p, sin_p, gq, gk, sink_p, gao, gmat, batch, seq)
    lo_p, h_p = _lru_prompt(xrp, ygp, cw, cb, wa_bd, wx_bd, ba, bx, lam, glo, batch, seq, 256)
    x1p, hn_p = _outproj(xp2, ao_p, lo_p, w_out_bf, g_ffn, 256)

    bb = 8
    qs, ks, vs, xrs, ygs = _inproj(xs2, g_mix, w_in_bf, 256)
    cos_sb = jnp.tile(cos_s, (bb, 1))
    sin_sb = jnp.tile(sin_s, (bb, 1))
    ck = cache_k[l].reshape(dbatch, WINDOW, KV_WIDTH)
    cv = cache_v[l].reshape(dbatch, WINDOW, KV_WIDTH)
    ao_s, k_s, v_s = _attn_sample(qs, ks, vs, ck, cv, cos_sb, sin_sb, gq, gk, sink_s, gao, gmat, dbatch, dseq, bb)
    cpad = jnp.pad(state_conv[l], ((0, 0), (dseq - (CONV_W - 1), 0), (0, 0))).reshape(n_s, LRU_WIDTH)
    h0rows = jnp.repeat(state_h[l], dseq, axis=0)
    lo_s, h_s = _lru_sample(xrs, ygs, cpad, h0rows, cw, cb, wa_bd, wx_bd, ba, bx, lam, glo, dseq, 256)
    x1s, hn_s = _outproj(xs2, ao_s, lo_s, w_out_bf, g_ffn, 256)

    tm = 512
    outs = []
    for x1, hn in ((x1p, hn_p), (x1s, hn_s)):
        s1m, e1z, t0, e0 = _route(hn, wq_t, keys, tm)
        outs.append(_experts(hn, u_bf, vt_bf, s1m, e1z, t0, e0, x1, tm, 512))
    y_p = outs[0].reshape(batch, seq, D_MODEL)
    y_s = outs[1].reshape(dbatch, dseq, D_MODEL)

    kv_shape = (N_KV_HEADS, HEAD_DIM)
    k_prompt = k_last.reshape(1, batch, WINDOW, *kv_shape)
    v_prompt = vp.reshape(batch, seq, *kv_shape)[:, -WINDOW:][None]
    conv_prompt = xrp.reshape(batch, seq, LRU_WIDTH)[:, -(CONV_W - 1):][None]
    h_prompt = h_p.reshape(1, batch, LRU_WIDTH)
    k_sample = k_s.reshape(1, dbatch, WINDOW, *kv_shape)
    v_sample = v_s.reshape(1, dbatch, WINDOW, *kv_shape)
    conv_sample = xrs.reshape(dbatch, dseq, LRU_WIDTH)[:, -(CONV_W - 1):][None]
    h_sample = h_s.reshape(dbatch, dseq, LRU_WIDTH)[:, -1][None]
    return (y_p, y_s, k_prompt, v_prompt, conv_prompt, h_prompt, k_sample, v_sample, conv_sample, h_sample)
```

```python
import functools
import math

import jax
import jax.numpy as jnp
import numpy as np
from jax import lax
from jax.experimental import pallas as pl
from jax.experimental.pallas import tpu as pltpu

F32 = jnp.float32
BF16 = jnp.bfloat16

D_MODEL = 2048
HEAD_DIM = 64
N_Q_HEADS = 16
N_KV_HEADS = 4
GQA = N_Q_HEADS // N_KV_HEADS
ATTN_WIDTH = N_Q_HEADS * HEAD_DIM
KV_WIDTH = N_KV_HEADS * HEAD_DIM
WINDOW = 128
ROPE_THETA = 10000.0
LRU_WIDTH = D_MODEL - ATTN_WIDTH
LRU_BLOCK = 64
CONV_W = 4
LRU_C = 8.0
N_KEYS = 128
N_EXPERTS = N_KEYS * N_KEYS
PEER_HEADS = 8
PEER_TOPK = 16
D_KEY = 256
EPS = 1e-6
NEG_INF = -1e30

MXU_DIM = 256
LANES = 128
SUBLANES = 8
VMEM_LIMIT_BYTES = 60 * 1024 * 1024


def _cparams(sem, flags=None):
    return pltpu.CompilerParams(dimension_semantics=sem, vmem_limit_bytes=VMEM_LIMIT_BYTES, flags=flags)


IN_CUTS = (0, ATTN_WIDTH, ATTN_WIDTH + KV_WIDTH, ATTN_WIDTH + 2 * KV_WIDTH,
           ATTN_WIDTH + 2 * KV_WIDTH + LRU_WIDTH, ATTN_WIDTH + 2 * KV_WIDTH + 2 * LRU_WIDTH)


def _inproj_body(x_ref, g_ref, w_ref, q_ref, k_ref, v_ref, xr_ref, yg_ref):
    x = x_ref[...]
    ms = jnp.mean(x * x, axis=-1, keepdims=True)
    hn = (x * lax.rsqrt(ms + EPS) * g_ref[...]).astype(BF16)
    for o_ref, lo, hi in zip((q_ref, k_ref, v_ref, xr_ref, yg_ref), IN_CUTS[:-1], IN_CUTS[1:]):
        o_ref[...] = jnp.dot(hn, w_ref[:, lo:hi], preferred_element_type=F32)


def _inproj(x, g, w_bf, tm):
    n = x.shape[0]
    widths = [hi - lo for lo, hi in zip(IN_CUTS[:-1], IN_CUTS[1:])]
    return pl.pallas_call(
        _inproj_body,
        grid=(n // tm,),
        in_specs=[
            pl.BlockSpec((tm, D_MODEL), lambda i: (i, 0)),
            pl.BlockSpec((1, D_MODEL), lambda i: (0, 0)),
            pl.BlockSpec((D_MODEL, IN_CUTS[-1]), lambda i: (0, 0)),
        ],
        out_specs=[pl.BlockSpec((tm, w), lambda i: (i, 0)) for w in widths],
        out_shape=[jax.ShapeDtypeStruct((n, w), F32) for w in widths],
        compiler_params=_cparams(("arbitrary",)),
        name="inproj",
    )(x, g, w_bf)


def _group_mean_sq(x, gmat):
    x2 = x * x
    hi = x2.astype(BF16)
    lo = (x2 - hi.astype(F32)).astype(BF16)
    cols = []
    for c in range(x.shape[1] // MXU_DIM):
        sl = slice(c * MXU_DIM, (c + 1) * MXU_DIM)
        cols.append(jnp.dot(hi[:, sl], gmat, preferred_element_type=F32)
                    + jnp.dot(lo[:, sl], gmat, preferred_element_type=F32))
    return cols[0] if len(cols) == 1 else jnp.concatenate(cols, axis=1)


def _tile_lanes(t, reps):
    return t if reps == 1 else jnp.concatenate([t] * reps, axis=1)


def _head_norm_rope(x, gain, cos2, sin2, gmat):
    width = x.shape[1]
    y = x * lax.rsqrt(_group_mean_sq(x, gmat) + EPS) * gain
    lane = lax.broadcasted_iota(jnp.int32, y.shape, 1)
    first_half = (lane & (HEAD_DIM // 2)) == 0
    partner = jnp.where(first_half,
                        pltpu.roll(y, width - HEAD_DIM // 2, axis=1),
                        pltpu.roll(y, HEAD_DIM // 2, axis=1))
    reps = width // LANES
    return y * _tile_lanes(cos2, reps) + partner * _tile_lanes(sin2, reps)


def _sink_softmax(lg, mask, sink):
    lg = jnp.where(mask, lg, NEG_INF)
    m = jnp.maximum(jnp.max(lg, axis=-1, keepdims=True), sink)
    e = jnp.exp(lg - m)
    return e / (jnp.sum(e, axis=-1, keepdims=True) + jnp.exp(sink - m))


def _rms_rows(x, gain):
    ms = jnp.mean(x * x, axis=-1, keepdims=True)
    return x * lax.rsqrt(ms + EPS) * gain


def _attn_prompt_body(q_ref, k_ref, v_ref, cos_ref, sin_ref, gq_ref, gk_ref, sink_ref, gao_ref, gmat_ref,
                      ao_ref, klast_ref, kprev, vprev, *, nb):
    n = pl.program_id(1)
    blk = WINDOW

    @pl.when(n == 0)
    def _():
        kprev[...] = jnp.zeros_like(kprev)
        vprev[...] = jnp.zeros_like(vprev)

    gmat = gmat_ref[...]
    cos2 = cos_ref[...]
    sin2 = sin_ref[...]
    q = _head_norm_rope(q_ref[...], gq_ref[...], cos2, sin2, gmat) * (HEAD_DIM ** -0.5)
    k = _head_norm_rope(k_ref[...], gk_ref[...], cos2, sin2, gmat)
    v = v_ref[...]
    kc = jnp.concatenate([kprev[...], k], axis=0).astype(BF16)
    vc = jnp.concatenate([vprev[...], v], axis=0).astype(BF16)
    qb = q.astype(BF16)

    rows = GQA * blk
    i = lax.broadcasted_iota(jnp.int32, (rows, 2 * blk), 0) & (blk - 1)
    j = lax.broadcasted_iota(jnp.int32, (rows, 2 * blk), 1)
    mask = (j > i) & (j <= i + blk) & ((n > 0) | (j >= blk))

    pieces = []
    for g in range(N_KV_HEADS):
        qg = jnp.concatenate(
            [qb[:, (g * GQA + h) * HEAD_DIM:(g * GQA + h + 1) * HEAD_DIM] for h in range(GQA)], axis=0)
        kg = kc[:, g * HEAD_DIM:(g + 1) * HEAD_DIM]
        vg = vc[:, g * HEAD_DIM:(g + 1) * HEAD_DIM]
        lg = lax.dot_general(qg, kg, (((1,), (1,)), ((), ())), preferred_element_type=F32)
        p = _sink_softmax(lg, mask, sink_ref[g])
        og = jnp.dot(p.astype(BF16), vg, preferred_element_type=F32)
        pieces.extend(og[h * blk:(h + 1) * blk] for h in range(GQA))
    ao = jnp.concatenate(pieces, axis=1)
    ao_ref[...] = _rms_rows(ao, gao_ref[...]).astype(BF16)

    kprev[...] = k
    vprev[...] = v

    @pl.when(n == nb - 1)
    def _():
        klast_ref[0] = k


def _attn_prompt(q, k, v, cos2, sin2, gq, gk, sink_col, gao, gmat, batch, seq):
    nb = seq // WINDOW
    row = lambda b, n: (b * nb + n, 0)
    const2 = lambda b, n: (0, 0)
    return pl.pallas_call(
        functools.partial(_attn_prompt_body, nb=nb),
        grid=(batch, nb),
        in_specs=[
            pl.BlockSpec((WINDOW, ATTN_WIDTH), row),
            pl.BlockSpec((WINDOW, KV_WIDTH), row),
            pl.BlockSpec((WINDOW, KV_WIDTH), row),
            pl.BlockSpec((WINDOW, LANES), lambda b, n: (n, 0)),
            pl.BlockSpec((WINDOW, LANES), lambda b, n: (n, 0)),
            pl.BlockSpec((1, ATTN_WIDTH), const2),
            pl.BlockSpec((1, KV_WIDTH), const2),
            pl.BlockSpec((N_KV_HEADS, GQA * WINDOW, 1), lambda b, n: (0, 0, 0)),
            pl.BlockSpec((1, ATTN_WIDTH), const2),
            pl.BlockSpec((MXU_DIM, MXU_DIM), const2),
        ],
        out_specs=[
            pl.BlockSpec((WINDOW, ATTN_WIDTH), row),
            pl.BlockSpec((1, WINDOW, KV_WIDTH), lambda b, n: (b, 0, 0)),
        ],
        out_shape=[
            jax.ShapeDtypeStruct((batch * seq, ATTN_WIDTH), BF16),
            jax.ShapeDtypeStruct((batch, WINDOW, KV_WIDTH), F32),
        ],
        scratch_shapes=[pltpu.VMEM((WINDOW, KV_WIDTH), F32), pltpu.VMEM((WINDOW, KV_WIDTH), F32)],
        compiler_params=_cparams(("arbitrary", "arbitrary")),
        name="attn_prompt",
    )(q, k, v, cos2, sin2, gq, gk, sink_col, gao, gmat)


def _attn_sample_body(q_ref, k_ref, v_ref, ck_ref, cv_ref, cos_ref, sin_ref, gq_ref, gk_ref, sink_ref, gao_ref,
                      gmat_ref, ao_ref, ks_ref, vs_ref, *, bb, s):
    gmat = gmat_ref[...]
    cos2 = cos_ref[...]
    sin2 = sin_ref[...]
    q = _head_norm_rope(q_ref[...], gq_ref[...], cos2, sin2, gmat) * (HEAD_DIM ** -0.5)
    k = _head_norm_rope(k_ref[...], gk_ref[...], cos2, sin2, gmat)
    v = v_ref[...]
    ck = ck_ref[...]
    cv = cv_ref[...]
    k3 = k.reshape(bb, s, KV_WIDTH)
    v3 = v.reshape(bb, s, KV_WIDTH)
    ks_ref[:, :WINDOW - s, :] = ck[:, s:, :]
    ks_ref[:, WINDOW - s:, :] = k3
    vs_ref[:, :WINDOW - s, :] = cv[:, s:, :]
    vs_ref[:, WINDOW - s:, :] = v3

    q3 = q.astype(BF16).reshape(bb, s, ATTN_WIDTH)
    ckb = ck.astype(BF16)
    cvb = cv.astype(BF16)
    k3b = k3.astype(BF16)
    v3b = v3.astype(BF16)
    rows = GQA * s
    qi = lax.broadcasted_iota(jnp.int32, (bb, rows, WINDOW), 1) & (s - 1)
    tc = lax.broadcasted_iota(jnp.int32, (bb, rows, WINDOW), 2)
    mask_c = tc > qi
    qi2 = lax.broadcasted_iota(jnp.int32, (bb, rows, s), 1) & (s - 1)
    tn = lax.broadcasted_iota(jnp.int32, (bb, rows, s), 2)
    mask_n = tn <= qi2

    pieces = []
    for g in range(N_KV_HEADS):
        qg = jnp.concatenate(
            [q3[:, :, (g * GQA + h) * HEAD_DIM:(g * GQA + h + 1) * HEAD_DIM] for h in range(GQA)], axis=1)
        sl = slice(g * HEAD_DIM, (g + 1) * HEAD_DIM)
        lc = jnp.einsum("bqd,bkd->bqk", qg, ckb[:, :, sl], preferred_element_type=F32)
        ln = jnp.einsum("bqd,bkd->bqk", qg, k3b[:, :, sl], preferred_element_type=F32)
        lc = jnp.where(mask_c, lc, NEG_INF)
        ln = jnp.where(mask_n, ln, NEG_INF)
        sink = sink_ref[g]
        m = jnp.maximum(jnp.maximum(jnp.max(lc, axis=-1, keepdims=True), jnp.max(ln, axis=-1, keepdims=True)), sink)
        ec = jnp.exp(lc - m)
        en = jnp.exp(ln - m)
        den = jnp.sum(ec, axis=-1, keepdims=True) + jnp.sum(en, axis=-1, keepdims=True) + jnp.exp(sink - m)
        og = (jnp.einsum("bqk,bkd->bqd", (ec / den).astype(BF16), cvb[:, :, sl], preferred_element_type=F32)
              + jnp.einsum("bqk,bkd->bqd", (en / den).astype(BF16), v3b[:, :, sl], preferred_element_type=F32))
        pieces.extend(og[:, h * s:(h + 1) * s, :] for h in range(GQA))
    ao = jnp.concatenate(pieces, axis=2).reshape(bb * s, ATTN_WIDTH)
    ao_ref[...] = _rms_rows(ao, gao_ref[...]).astype(BF16)


def _attn_sample(q, k, v, ck, cv, cos2, sin2, gq, gk, sink_col, gao, gmat, batch, s, bb):
    rows = bb * s
    row = lambda i: (i, 0)
    const2 = lambda i: (0, 0)
    b3 = lambda i: (i, 0, 0)
    return pl.pallas_call(
        functools.partial(_attn_sample_body, bb=bb, s=s),
        grid=(batch // bb,),
        in_specs=[
            pl.BlockSpec((rows, ATTN_WIDTH), row),
            pl.BlockSpec((rows, KV_WIDTH), row),
            pl.BlockSpec((rows, KV_WIDTH), row),
            pl.BlockSpec((bb, WINDOW, KV_WIDTH), b3),
            pl.BlockSpec((bb, WINDOW, KV_WIDTH), b3),
            pl.BlockSpec((rows, LANES), const2),
            pl.BlockSpec((rows, LANES), const2),
            pl.BlockSpec((1, ATTN_WIDTH), const2),
            pl.BlockSpec((1, KV_WIDTH), const2),
            pl.BlockSpec((N_KV_HEADS, GQA * s, 1), lambda i: (0, 0, 0)),
            pl.BlockSpec((1, ATTN_WIDTH), const2),
            pl.BlockSpec((MXU_DIM, MXU_DIM), const2),
        ],
        out_specs=[
            pl.BlockSpec((rows, ATTN_WIDTH), row),
            pl.BlockSpec((bb, WINDOW, KV_WIDTH), b3),
            pl.BlockSpec((bb, WINDOW, KV_WIDTH), b3),
        ],
        out_shape=[
            jax.ShapeDtypeStruct((batch * s, ATTN_WIDTH), BF16),
            jax.ShapeDtypeStruct((batch, WINDOW, KV_WIDTH), F32),
            jax.ShapeDtypeStruct((batch, WINDOW, KV_WIDTH), F32),
        ],
        compiler_params=_cparams(("arbitrary",)),
        name="attn_sample",
    )(q, k, v, ck, cv, cos2, sin2, gq, gk, sink_col, gao, gmat)


def _lru_gates(xc, wa_ref, wx_ref, ba, bx, lam):
    xb = xc.astype(BF16)
    r_cols, i_cols = [], []
    for c in range(LRU_WIDTH // MXU_DIM):
        sl = slice(c * MXU_DIM, (c + 1) * MXU_DIM)
        r_cols.append(jnp.dot(xb[:, sl], wa_ref[c], preferred_element_type=F32))
        i_cols.append(jnp.dot(xb[:, sl], wx_ref[c], preferred_element_type=F32))
    r = 1.0 / (1.0 + jnp.exp(-(jnp.concatenate(r_cols, axis=1) + ba)))
    ig = 1.0 / (1.0 + jnp.exp(-(jnp.concatenate(i_cols, axis=1) + bx)))
    neg_lam = -lam
    softplus = jnp.maximum(neg_lam, 0.0) + jnp.log1p(jnp.exp(-jnp.abs(neg_lam)))
    log_a = -LRU_C * r * softplus
    a = jnp.exp(log_a)
    u = jnp.sqrt(-jnp.tanh(log_a) * (a * a + 1.0)) * (ig * xc)
    return a, u


def _segment_scan(a, u, seg):
    t = lax.broadcasted_iota(jnp.int32, a.shape, 0) & (seg - 1)
    d = 1
    while d < seg:
        valid = t >= d
        u = jnp.where(valid, a * pltpu.roll(u, d, axis=0) + u, u)
        a = jnp.where(valid, a * pltpu.roll(a, d, axis=0), a)
        d *= 2
    return a, u


def _lru_prompt_body(xr_ref, yg_ref, cw_ref, cb_ref, wa_ref, wx_ref, ba_ref, bx_ref, lam_ref, glo_ref,
                     lo_ref, hlast_ref, prev8, hcarry, *, tb):
    t_blk = pl.program_id(1)

    @pl.when(t_blk == 0)
    def _():
        prev8[...] = jnp.zeros_like(prev8)
        hcarry[...] = jnp.zeros_like(hcarry)

    xr = xr_ref[...]
    p8 = prev8[...]
    row8 = lax.broadcasted_iota(jnp.int32, p8.shape, 0)
    xc = xr * cw_ref[CONV_W - 1:CONV_W, :] + cb_ref[...]
    for kshift in range(1, CONV_W):
        rolled = pltpu.roll(xr, kshift, axis=0)
        top = jnp.where(row8 < kshift, pltpu.roll(p8, kshift, axis=0), rolled[:SUBLANES])
        shifted = jnp.concatenate([top, rolled[SUBLANES:]], axis=0)
        xc = xc + shifted * cw_ref[CONV_W - 1 - kshift:CONV_W - kshift, :]
    a, u = _lru_gates(xc, wa_ref, wx_ref, ba_ref[...], bx_ref[...], lam_ref[...])
    acum, hloc = _segment_scan(a, u, tb)
    h = hloc + acum * hcarry[...]
    y = h * jax.nn.gelu(yg_ref[...])
    lo_ref[...] = _rms_rows(y, glo_ref[...]).astype(BF16)
    hlast = h[tb - 1:tb, :]
    hcarry[...] = hlast
    prev8[...] = xr[tb - SUBLANES:, :]
    hlast_ref[0] = hlast


def _lru_prompt(xr, yg, cw, cb, wa_bd, wx_bd, ba, bx, lam, glo, batch, seq, tb):
    nt = seq // tb
    row = lambda b, t: (b * nt + t, 0)
    const2 = lambda b, t: (0, 0)
    const3 = lambda b, t: (0, 0, 0)
    nchunk = LRU_WIDTH // MXU_DIM
    return pl.pallas_call(
        functools.partial(_lru_prompt_body, tb=tb),
        grid=(batch, nt),
        in_specs=[
            pl.BlockSpec((tb, LRU_WIDTH), row),
            pl.BlockSpec((tb, LRU_WIDTH), row),
            pl.BlockSpec((CONV_W, LRU_WIDTH), const2),
            pl.BlockSpec((1, LRU_WIDTH), const2),
            pl.BlockSpec((nchunk, MXU_DIM, MXU_DIM), const3),
            pl.BlockSpec((nchunk, MXU_DIM, MXU_DIM), const3),
            pl.BlockSpec((1, LRU_WIDTH), const2),
            pl.BlockSpec((1, LRU_WIDTH), const2),
            pl.BlockSpec((1, LRU_WIDTH), const2),
            pl.BlockSpec((1, LRU_WIDTH), const2),
        ],
        out_specs=[
            pl.BlockSpec((tb, LRU_WIDTH), row),
            pl.BlockSpec((1, 1, LRU_WIDTH), lambda b, t: (b, 0, 0)),
        ],
        out_shape=[
            jax.ShapeDtypeStruct((batch * seq, LRU_WIDTH), BF16),
            jax.ShapeDtypeStruct((batch, 1, LRU_WIDTH), F32),
        ],
        scratch_shapes=[pltpu.VMEM((SUBLANES, LRU_WIDTH), F32), pltpu.VMEM((1, LRU_WIDTH), F32)],
        compiler_params=_cparams(("arbitrary", "arbitrary")),
        name="lru_prompt",
    )(xr, yg, cw, cb, wa_bd, wx_bd, ba, bx, lam, glo)


def _lru_sample_body(xr_ref, yg_ref, cpad_ref, h0_ref, cw_ref, cb_ref, wa_ref, wx_ref, ba_ref, bx_ref, lam_ref,
                     glo_ref, lo_ref, h_ref, *, s):
    xr = xr_ref[...]
    cpad = cpad_ref[...]
    rows = xr.shape[0]
    t = lax.broadcasted_iota(jnp.int32, xr.shape, 0) & (s - 1)
    xc = xr * cw_ref[CONV_W - 1:CONV_W, :] + cb_ref[...]
    for kshift in range(1, CONV_W):
        shifted = jnp.where(t < kshift, pltpu.roll(cpad, rows - s + kshift, axis=0), pltpu.roll(xr, kshift, axis=0))
        xc = xc + shifted * cw_ref[CONV_W - 1 - kshift:CONV_W - kshift, :]
    a, u = _lru_gates(xc, wa_ref, wx_ref, ba_ref[...], bx_ref[...], lam_ref[...])
    acum, hloc = _segment_scan(a, u, s)
    h = hloc + acum * h0_ref[...]
    h_ref[...] = h
    y = h * jax.nn.gelu(yg_ref[...])
    lo_ref[...] = _rms_rows(y, glo_ref[...]).astype(BF16)


def _lru_sample(xr, yg, cpad, h0rows, cw, cb, wa_bd, wx_bd, ba, bx, lam, glo, s, tb):
    n = xr.shape[0]
    row = lambda i: (i, 0)
    const2 = lambda i: (0, 0)
    const3 = lambda i: (0, 0, 0)
    nchunk = LRU_WIDTH // MXU_DIM
    return pl.pallas_call(
        functools.partial(_lru_sample_body, s=s),
        grid=(n // tb,),
        in_specs=[
            pl.BlockSpec((tb, LRU_WIDTH), row),
            pl.BlockSpec((tb, LRU_WIDTH), row),
            pl.BlockSpec((tb, LRU_WIDTH), row),
            pl.BlockSpec((tb, LRU_WIDTH), row),
            pl.BlockSpec((CONV_W, LRU_WIDTH), const2),
            pl.BlockSpec((1, LRU_WIDTH), const2),
            pl.BlockSpec((nchunk, MXU_DIM, MXU_DIM), const3),
            pl.BlockSpec((nchunk, MXU_DIM, MXU_DIM), const3),
            pl.BlockSpec((1, LRU_WIDTH), const2),
            pl.BlockSpec((1, LRU_WIDTH), const2),
            pl.BlockSpec((1, LRU_WIDTH), const2),
            pl.BlockSpec((1, LRU_WIDTH), const2),
        ],
        out_specs=[pl.BlockSpec((tb, LRU_WIDTH), row), pl.BlockSpec((tb, LRU_WIDTH), row)],
        out_shape=[jax.ShapeDtypeStruct((n, LRU_WIDTH), BF16), jax.ShapeDtypeStruct((n, LRU_WIDTH), F32)],
        compiler_params=_cparams(("arbitrary",)),
        name="lru_sample",
    )(xr, yg, cpad, h0rows, cw, cb, wa_bd, wx_bd, ba, bx, lam, glo)


def _outproj_body(x_ref, ao_ref, lo_ref, w_ref, g_ref, x1_ref, hnt_ref):
    mixed = (jnp.dot(ao_ref[...], w_ref[:ATTN_WIDTH, :], preferred_element_type=F32)
             + jnp.dot(lo_ref[...], w_ref[ATTN_WIDTH:, :], preferred_element_type=F32))
    x1 = x_ref[...] + mixed
    x1_ref[...] = x1
    hnt_ref[...] = _rms_rows(x1, g_ref[...]).T.astype(BF16)


def _outproj(x, ao, lo, w_bf, g, tm):
    n = x.shape[0]
    row = lambda i: (i, 0)
    const2 = lambda i: (0, 0)
    return pl.pallas_call(
        _outproj_body,
        grid=(n // tm,),
        in_specs=[
            pl.BlockSpec((tm, D_MODEL), row),
            pl.BlockSpec((tm, ATTN_WIDTH), row),
            pl.BlockSpec((tm, LRU_WIDTH), row),
            pl.BlockSpec((D_MODEL, D_MODEL), const2),
            pl.BlockSpec((1, D_MODEL), const2),
        ],
        out_specs=[pl.BlockSpec((tm, D_MODEL), row), pl.BlockSpec((D_MODEL, tm), lambda i: (0, i))],
        out_shape=[jax.ShapeDtypeStruct((n, D_MODEL), F32), jax.ShapeDtypeStruct((D_MODEL, n), BF16)],
        compiler_params=_cparams(("arbitrary",)),
        name="outproj",
    )(x, ao, lo, w_bf, g)


PEER_CAND_COLS = tuple(min(PEER_TOPK, (PEER_TOPK + 1) // (a + 1)) for a in range(PEER_TOPK))
ROUTE_LANES = 128


def _extract_max(cur, rowf, big):
    m = jnp.max(cur, axis=0, keepdims=True)
    idx = jnp.min(jnp.where(cur == m, rowf, big), axis=0, keepdims=True)
    hit = rowf == idx
    return m, hit


def _top16(s):
    rowf = lax.broadcasted_iota(jnp.int32, s.shape, 0).astype(F32)
    cur = s
    kept = jnp.full_like(s, -jnp.inf)
    vals = []
    for _ in range(PEER_TOPK):
        m, hit = _extract_max(cur, rowf, float(N_KEYS))
        vals.append(m)
        kept = jnp.where(hit, m, kept)
        cur = jnp.where(hit, -jnp.inf, cur)
    return vals, kept


def _route_body(hnt_ref, wq_ref, keys_ref, s1m_ref, e1z_ref, t0_ref, e0_ref, s_scr, *, tm):
    qt = jnp.dot(wq_ref[...], hnt_ref[...], preferred_element_type=F32)
    half = D_KEY // 2
    for hc in range(2 * PEER_HEADS):
        qhc = qt[hc * half:(hc + 1) * half, :].astype(BF16)
        s_scr[hc] = jnp.dot(keys_ref[hc], qhc, preferred_element_type=F32)

    def step(it, carry):
        h = it // (tm // ROUTE_LANES)
        c = it % (tm // ROUTE_LANES)
        lanes = pl.ds(pl.multiple_of(c * ROUTE_LANES, ROUTE_LANES), ROUTE_LANES)
        v0, s0m = _top16(s_scr[2 * h, :, lanes])
        v1, s1m = _top16(s_scr[2 * h + 1, :, lanes])
        v0_hi = jnp.concatenate(v0[SUBLANES:], axis=0)
        v1_all = jnp.concatenate(v1, axis=0)
        v1_lo = v1_all[:SUBLANES]
        row8 = lax.broadcasted_iota(jnp.int32, v1_lo.shape, 0)
        groups = [v1_all + v0[0]]
        for a in range(1, SUBLANES):
            groups.append(jnp.where(row8 < PEER_CAND_COLS[a], v1_lo + v0[a], -jnp.inf))
        groups.append(v0_hi + v1[0])
        cand = jnp.concatenate(groups, axis=0)
        rowf = lax.broadcasted_iota(jnp.int32, cand.shape, 0).astype(F32)
        best = []
        for _ in range(PEER_TOPK + 1):
            m, hit = _extract_max(cand, rowf, float(cand.shape[0]))
            best.append(m)
            cand = jnp.where(hit, -jnp.inf, cand)
        mx = best[0]
        z = jnp.exp(best[0] - mx)
        for r in range(1, PEER_TOPK):
            z = z + jnp.exp(best[r] - mx)
        thr = 0.5 * (best[PEER_TOPK - 1] + best[PEER_TOPK])
        s1m_ref[h, :, lanes] = s1m
        e1z_ref[h, :, lanes] = jnp.exp(s1m - v1[0]) / z
        t0_ref[h, :, lanes] = thr - s0m
        e0_ref[h, :, lanes] = jnp.exp(s0m - v0[0])
        return carry

    lax.fori_loop(0, PEER_HEADS * (tm // ROUTE_LANES), step, 0)


def _route(hnt, wq_t, keys, tm):
    n = hnt.shape[1]
    blk = lambda i: (0, 0, i)
    shape = jax.ShapeDtypeStruct((PEER_HEADS, N_KEYS, n), F32)
    spec = pl.BlockSpec((PEER_HEADS, N_KEYS, tm), blk)
    return pl.pallas_call(
        functools.partial(_route_body, tm=tm),
        grid=(n // tm,),
        in_specs=[
            pl.BlockSpec((D_MODEL, tm), lambda i: (0, i)),
            pl.BlockSpec((PEER_HEADS * D_KEY, D_MODEL), lambda i: (0, 0)),
            pl.BlockSpec((2 * PEER_HEADS, N_KEYS, D_KEY // 2), lambda i: (0, 0, 0)),
        ],
        out_specs=[spec, spec, spec, spec],
        out_shape=[shape, shape, shape, shape],
        scratch_shapes=[pltpu.VMEM((2 * PEER_HEADS, N_KEYS, tm), F32)],
        compiler_params=_cparams(("arbitrary",)),
        name="peer_route",
    )(hnt, wq_t, keys)


EXPERT_LANES = 256
EXPERT_BLOCK = 512
EXPERT_ROWS = EXPERT_BLOCK // N_KEYS
U_KCHUNK = 512
ROWS_WITH_DOT1 = 3


def _routing_weight(s1m_ref, e1z_ref, t0_ref, e0_ref, k, ii, lanes):
    width = lanes.stop - lanes.start
    trow = [jnp.broadcast_to(t0_ref[h, k, ii:ii + 1, lanes], (SUBLANES, width)) for h in range(PEER_HEADS)]
    e0row = [jnp.broadcast_to(e0_ref[h, k, ii:ii + 1, lanes], (SUBLANES, width)) for h in range(PEER_HEADS)]
    out = []
    for j in range(0, N_KEYS, SUBLANES):
        w = None
        for h in range(PEER_HEADS):
            term = jnp.where(s1m_ref[h, j:j + SUBLANES, lanes] >= trow[h], e1z_ref[h, j:j + SUBLANES, lanes], 0.0) * e0row[h]
            w = term if w is None else w + term
        out.append(w)
    return jnp.concatenate(out, axis=0)


def _experts_body(hnt_ref, ua_ref, ub_ref, vtp_ref, vta_ref, s1m_ref, e1z_ref, t0_ref, e0_ref, x1_ref, y_ref,
                  acc, act, wbuf, pt_a, pt_b, *, tm, nk):
    k = pl.program_id(1)
    kt = jnp.minimum(k, nk - 1)
    nlane = tm // EXPERT_LANES

    def weights(blk, row_lo, row_hi):
        for ii in range(blk * EXPERT_ROWS + row_lo, blk * EXPERT_ROWS + row_hi):
            rows = slice(ii * N_KEYS, (ii + 1) * N_KEYS)
            for c in range(nlane):
                lanes = slice(c * EXPERT_LANES, (c + 1) * EXPERT_LANES)
                wbuf[rows, lanes] = _routing_weight(s1m_ref, e1z_ref, t0_ref, e0_ref, kt, ii, lanes)

    def weights_tile(ii, c):
        rows = slice(ii * N_KEYS, (ii + 1) * N_KEYS)
        lanes = slice(c * EXPERT_LANES, (c + 1) * EXPERT_LANES)
        wbuf[rows, lanes] = _routing_weight(s1m_ref, e1z_ref, t0_ref, e0_ref, kt, ii, lanes)

    def dot1_piece(blk, c):
        u_ref = (ua_ref, ub_ref)[blk]
        lanes = slice(c * EXPERT_LANES, (c + 1) * EXPERT_LANES)
        a = None
        for kc in range(D_MODEL // U_KCHUNK):
            d = jnp.dot(u_ref[kc], hnt_ref[kc * U_KCHUNK:(kc + 1) * U_KCHUNK, lanes], preferred_element_type=F32)
            a = d if a is None else a + d
        act[:, lanes] = a

    def dot1(blk):
        for c in range(nlane):
            for ii in range(blk * EXPERT_ROWS, blk * EXPERT_ROWS + ROWS_WITH_DOT1):
                weights_tile(ii, c)
            dot1_piece(blk, c)

    def weighted_act(blk, pt):
        for ii in range(EXPERT_ROWS):
            rows = slice(ii * N_KEYS, (ii + 1) * N_KEYS)
            wrows = slice((blk * EXPERT_ROWS + ii) * N_KEYS, (blk * EXPERT_ROWS + ii + 1) * N_KEYS)
            for c in range(nlane):
                lanes = slice(c * EXPERT_LANES, (c + 1) * EXPERT_LANES)
                pt[rows, lanes] = (wbuf[wrows, lanes] * jax.nn.gelu(act[rows, lanes])).astype(BF16)

    def dot2(vt_ref, pt):
        acc[...] += jnp.dot(vt_ref[...], pt[...], preferred_element_type=F32)

    @pl.when(k == 0)
    def _():
        acc[...] = jnp.zeros_like(acc)
        pt_b[...] = jnp.zeros_like(pt_b)

    @pl.when(k < nk)
    def _():
        dot1(0)

    weights(0, ROWS_WITH_DOT1, EXPERT_ROWS)
    weighted_act(0, pt_a)
    dot2(vtp_ref, pt_b)

    @pl.when(k < nk)
    def _():
        dot1(1)

    @pl.when(kt == k)
    def _():
        weights(1, ROWS_WITH_DOT1, EXPERT_ROWS)
        weighted_act(1, pt_b)
        dot2(vta_ref, pt_a)

    @pl.when(k == nk)
    def _():
        y_ref[...] = x1_ref[...] + acc[...].T


def _experts(hnt, u_bf, vt_bf, s1m, e1z, t0, e0, x1, tm):
    n = hnt.shape[1]
    assert 2 * EXPERT_BLOCK == SUBLANES * N_KEYS
    nk = N_EXPERTS // (2 * EXPERT_BLOCK)
    nblk = N_EXPERTS // EXPERT_BLOCK
    tok = lambda t, k: (t, 0)
    once = pl.Buffered(1)
    small = pl.BlockSpec((PEER_HEADS, N_KEYS, tm), lambda t, k: (0, 0, t), pipeline_mode=once)
    tiled = pl.BlockSpec((PEER_HEADS, N_KEYS // SUBLANES, SUBLANES, tm), lambda t, k: (0, 0, 0, t), pipeline_mode=once)
    t0 = t0.reshape(PEER_HEADS, N_KEYS // SUBLANES, SUBLANES, n)
    e0 = e0.reshape(PEER_HEADS, N_KEYS // SUBLANES, SUBLANES, n)
    return pl.pallas_call(
        functools.partial(_experts_body, tm=tm, nk=nk),
        grid=(n // tm, nk + 1),
        in_specs=[
            pl.BlockSpec((D_MODEL, tm), lambda t, k: (0, t), pipeline_mode=once),
            pl.BlockSpec((D_MODEL // U_KCHUNK, EXPERT_BLOCK, U_KCHUNK), lambda t, k: (0, 2 * jnp.minimum(k, nk - 1), 0)),
            pl.BlockSpec((D_MODEL // U_KCHUNK, EXPERT_BLOCK, U_KCHUNK), lambda t, k: (0, 2 * jnp.minimum(k, nk - 1) + 1, 0)),
            pl.BlockSpec((D_MODEL, EXPERT_BLOCK), lambda t, k: (0, jnp.maximum(2 * k - 1, 0))),
            pl.BlockSpec((D_MODEL, EXPERT_BLOCK), lambda t, k: (0, jnp.minimum(2 * k, nblk - 1))),
            small, small, tiled, tiled,
            pl.BlockSpec((tm, D_MODEL), tok, pipeline_mode=once),
        ],
        out_specs=pl.BlockSpec((tm, D_MODEL), tok),
        out_shape=jax.ShapeDtypeStruct((n, D_MODEL), F32),
        scratch_shapes=[
            pltpu.VMEM((D_MODEL, tm), F32),
            pltpu.VMEM((EXPERT_BLOCK, tm), F32),
            pltpu.VMEM((2 * EXPERT_BLOCK, tm), F32),
            pltpu.VMEM((EXPERT_BLOCK, tm), BF16),
            pltpu.VMEM((EXPERT_BLOCK, tm), BF16),
        ],
        compiler_params=_cparams(("arbitrary", "arbitrary")),
        name="peer_experts",
    )(hnt, u_bf, u_bf, vt_bf, vt_bf, s1m, e1z, t0, e0, x1)


def _rope_tables(pos):
    half = HEAD_DIM // 2
    inv = jnp.exp(-math.log(ROPE_THETA) * jnp.arange(half, dtype=F32) * (2.0 / HEAD_DIM))
    ang = pos[:, None] * inv[None, :]
    cos, sin = jnp.cos(ang), jnp.sin(ang)
    cos2 = jnp.concatenate([cos, cos, cos, cos], axis=1)
    sin2 = jnp.concatenate([-sin, sin, -sin, sin], axis=1)
    return cos2, sin2


def _block_diag(w):
    per = MXU_DIM // LRU_BLOCK
    w4 = w.reshape(LRU_WIDTH // MXU_DIM, per, LRU_BLOCK, LRU_BLOCK)
    eye = jnp.eye(per, dtype=w.dtype)
    return jnp.einsum("cpde,pq->cpdqe", w4, eye).reshape(LRU_WIDTH // MXU_DIM, MXU_DIM, MXU_DIM).astype(BF16)


def kernel(x_prompt, x_sample, cache_k, cache_v, state_conv, state_h, norm_mix_g, w_in, q_norm_g, k_norm_g, attn_sinks, conv_w, conv_b, w_rec_gate, b_rec_gate, w_in_gate, b_in_gate, lru_lambda, attn_out_g, lru_out_g, w_out, norm_ffn_g, w_peer_q, peer_sub_keys, peer_u, peer_v):
    depth = w_in.shape[0]
    assert depth == 1
    batch, seq, _ = x_prompt.shape
    dbatch, dseq, _ = x_sample.shape
    past_len = 8192
    l = 0

    w_in_bf = w_in[l].astype(BF16)
    w_out_bf = w_out[l].astype(BF16)
    wq_t = w_peer_q[l].T.astype(BF16)
    keys = peer_sub_keys[l].reshape(2 * PEER_HEADS, N_KEYS, D_KEY // 2).astype(BF16)
    u_bf = peer_u[l].reshape(N_EXPERTS, D_MODEL // U_KCHUNK, U_KCHUNK).transpose(1, 0, 2).astype(BF16)
    vt_bf = peer_v[l].T.astype(BF16)
    wa_bd = _block_diag(w_rec_gate[l])
    wx_bd = _block_diag(w_in_gate[l])
    ba = b_rec_gate[l].reshape(1, LRU_WIDTH)
    bx = b_in_gate[l].reshape(1, LRU_WIDTH)
    lam = lru_lambda[l].reshape(1, LRU_WIDTH)
    cw = conv_w[l]
    cb = conv_b[l].reshape(1, LRU_WIDTH)
    g_mix = norm_mix_g[l].reshape(1, D_MODEL)
    g_ffn = norm_ffn_g[l].reshape(1, D_MODEL)
    gq = jnp.tile(q_norm_g[l], N_Q_HEADS).reshape(1, ATTN_WIDTH)
    gk = jnp.tile(k_norm_g[l], N_KV_HEADS).reshape(1, KV_WIDTH)
    gao = attn_out_g[l].reshape(1, ATTN_WIDTH)
    glo = lru_out_g[l].reshape(1, LRU_WIDTH)
    gmat = jnp.asarray(np.kron(np.eye(MXU_DIM // HEAD_DIM), np.full((HEAD_DIM, HEAD_DIM), 1.0 / HEAD_DIM)), BF16)
    sinks = attn_sinks[l].astype(F32).reshape(N_KV_HEADS, GQA)
    sink_p = jnp.repeat(sinks, WINDOW, axis=1).reshape(N_KV_HEADS, GQA * WINDOW, 1)
    sink_s = jnp.repeat(sinks, dseq, axis=1).reshape(N_KV_HEADS, GQA * dseq, 1)
    cos_p, sin_p = _rope_tables(jnp.arange(seq, dtype=F32))
    cos_s, sin_s = _rope_tables(float(past_len) + jnp.arange(dseq, dtype=F32))

    n_p = batch * seq
    n_s = dbatch * dseq
    xp2 = x_prompt.reshape(n_p, D_MODEL)
    xs2 = x_sample.reshape(n_s, D_MODEL)

    qp, kp, vp, xrp, ygp = _inproj(xp2, g_mix, w_in_bf, 256)
    ao_p, k_last = _attn_prompt(qp, kp, vp, cos_p, sin_p, gq, gk, sink_p, gao, gmat, batch, seq)
    lo_p, h_p = _lru_prompt(xrp, ygp, cw, cb, wa_bd, wx_bd, ba, bx, lam, glo, batch, seq, 256)
    x1p, hn_p = _outproj(xp2, ao_p, lo_p, w_out_bf, g_ffn, 256)

    bb = 8
    qs, ks, vs, xrs, ygs = _inproj(xs2, g_mix, w_in_bf, 256)
    cos_sb = jnp.tile(cos_s, (bb, 1))
    sin_sb = jnp.tile(sin_s, (bb, 1))
    ck = cache_k[l].reshape(dbatch, WINDOW, KV_WIDTH)
    cv = cache_v[l].reshape(dbatch, WINDOW, KV_WIDTH)
    ao_s, k_s, v_s = _attn_sample(qs, ks, vs, ck, cv, cos_sb, sin_sb, gq, gk, sink_s, gao, gmat, dbatch, dseq, bb)
    cpad = jnp.pad(state_conv[l], ((0, 0), (dseq - (CONV_W - 1), 0), (0, 0))).reshape(n_s, LRU_WIDTH)
    h0rows = jnp.repeat(state_h[l], dseq, axis=0)
    lo_s, h_s = _lru_sample(xrs, ygs, cpad, h0rows, cw, cb, wa_bd, wx_bd, ba, bx, lam, glo, dseq, 256)
    x1s, hn_s = _outproj(xs2, ao_s, lo_s, w_out_bf, g_ffn, 256)

    tm = 512
    outs = []
    for x1, hn in ((x1p, hn_p), (x1s, hn_s)):
        s1m, e1z, t0, e0 = _route(hn, wq_t, keys, tm)
        outs.append(_experts(hn, u_bf, vt_bf, s1m, e1z, t0, e0, x1, tm))
    y_p = outs[0].reshape(batch, seq, D_MODEL)
    y_s = outs[1].reshape(dbatch, dseq, D_MODEL)

    kv_shape = (N_KV_HEADS, HEAD_DIM)
    k_prompt = k_last.reshape(1, batch, WINDOW, *kv_shape)
    v_prompt = vp.reshape(batch, seq, *kv_shape)[:, -WINDOW:][None]
    conv_prompt = xrp.reshape(batch, seq, LRU_WIDTH)[:, -(CONV_W - 1):][None]
    h_prompt = h_p.reshape(1, batch, LRU_WIDTH)
    k_sample = k_s.reshape(1, dbatch, WINDOW, *kv_shape)
    v_sample = v_s.reshape(1, dbatch, WINDOW, *kv_shape)
    conv_sample = xrs.reshape(dbatch, dseq, LRU_WIDTH)[:, -(CONV_W - 1):][None]
    h_sample = h_s.reshape(dbatch, dseq, LRU_WIDTH)[:, -1][None]
    return (y_p, y_s, k_prompt, v_prompt, conv_prompt, h_prompt, k_sample, v_sample, conv_sample, h_sample)
```

```python
import functools
import math

import jax
import jax.numpy as jnp
import numpy as np
from jax import lax
from jax.experimental import pallas as pl
from jax.experimental.pallas import tpu as pltpu

F32 = jnp.float32
BF16 = jnp.bfloat16

D_MODEL = 2048
HEAD_DIM = 64
N_Q_HEADS = 16
N_KV_HEADS = 4
GQA = N_Q_HEADS // N_KV_HEADS
ATTN_WIDTH = N_Q_HEADS * HEAD_DIM
KV_WIDTH = N_KV_HEADS * HEAD_DIM
WINDOW = 128
ROPE_THETA = 10000.0
LRU_WIDTH = D_MODEL - ATTN_WIDTH
LRU_BLOCK = 64
CONV_W = 4
LRU_C = 8.0
N_KEYS = 128
N_EXPERTS = N_KEYS * N_KEYS
PEER_HEADS = 8
PEER_TOPK = 16
D_KEY = 256
EPS = 1e-6
NEG_INF = -1e30

MXU_DIM = 256
LANES = 128
SUBLANES = 8
VMEM_LIMIT_BYTES = 60 * 1024 * 1024


def _cparams(sem, flags=None):
    return pltpu.CompilerParams(dimension_semantics=sem, vmem_limit_bytes=VMEM_LIMIT_BYTES, flags=flags)


IN_CUTS = (0, ATTN_WIDTH, ATTN_WIDTH + KV_WIDTH, ATTN_WIDTH + 2 * KV_WIDTH,
           ATTN_WIDTH + 2 * KV_WIDTH + LRU_WIDTH, ATTN_WIDTH + 2 * KV_WIDTH + 2 * LRU_WIDTH)


def _inproj_body(x_ref, g_ref, w_ref, q_ref, k_ref, v_ref, xr_ref, yg_ref):
    x = x_ref[...]
    ms = jnp.mean(x * x, axis=-1, keepdims=True)
    hn = (x * lax.rsqrt(ms + EPS) * g_ref[...]).astype(BF16)
    for o_ref, lo, hi in zip((q_ref, k_ref, v_ref, xr_ref, yg_ref), IN_CUTS[:-1], IN_CUTS[1:]):
        o_ref[...] = jnp.dot(hn, w_ref[:, lo:hi], preferred_element_type=F32)


def _inproj(x, g, w_bf, tm):
    n = x.shape[0]
    widths = [hi - lo for lo, hi in zip(IN_CUTS[:-1], IN_CUTS[1:])]
    return pl.pallas_call(
        _inproj_body,
        grid=(n // tm,),
        in_specs=[
            pl.BlockSpec((tm, D_MODEL), lambda i: (i, 0)),
            pl.BlockSpec((1, D_MODEL), lambda i: (0, 0)),
            pl.BlockSpec((D_MODEL, IN_CUTS[-1]), lambda i: (0, 0)),
        ],
        out_specs=[pl.BlockSpec((tm, w), lambda i: (i, 0)) for w in widths],
        out_shape=[jax.ShapeDtypeStruct((n, w), F32) for w in widths],
        compiler_params=_cparams(("arbitrary",)),
        name="inproj",
    )(x, g, w_bf)


def _group_mean_sq(x, gmat):
    x2 = x * x
    hi = x2.astype(BF16)
    lo = (x2 - hi.astype(F32)).astype(BF16)
    cols = []
    for c in range(x.shape[1] // MXU_DIM):
        sl = slice(c * MXU_DIM, (c + 1) * MXU_DIM)
        cols.append(jnp.dot(hi[:, sl], gmat, preferred_element_type=F32)
                    + jnp.dot(lo[:, sl], gmat, preferred_element_type=F32))
    return cols[0] if len(cols) == 1 else jnp.concatenate(cols, axis=1)


def _tile_lanes(t, reps):
    return t if reps == 1 else jnp.concatenate([t] * reps, axis=1)


def _head_norm_rope(x, gain, cos2, sin2, gmat):
    width = x.shape[1]
    y = x * lax.rsqrt(_group_mean_sq(x, gmat) + EPS) * gain
    lane = lax.broadcasted_iota(jnp.int32, y.shape, 1)
    first_half = (lane & (HEAD_DIM // 2)) == 0
    partner = jnp.where(first_half,
                        pltpu.roll(y, width - HEAD_DIM // 2, axis=1),
                        pltpu.roll(y, HEAD_DIM // 2, axis=1))
    reps = width // LANES
    return y * _tile_lanes(cos2, reps) + partner * _tile_lanes(sin2, reps)


def _sink_softmax(lg, mask, sink):
    lg = jnp.where(mask, lg, NEG_INF)
    m = jnp.maximum(jnp.max(lg, axis=-1, keepdims=True), sink)
    e = jnp.exp(lg - m)
    return e / (jnp.sum(e, axis=-1, keepdims=True) + jnp.exp(sink - m))


def _rms_rows(x, gain):
    ms = jnp.mean(x * x, axis=-1, keepdims=True)
    return x * lax.rsqrt(ms + EPS) * gain


def _attn_prompt_body(q_ref, k_ref, v_ref, cos_ref, sin_ref, gq_ref, gk_ref, sink_ref, gao_ref, gmat_ref,
                      ao_ref, klast_ref, kprev, vprev, *, nb):
    n = pl.program_id(1)
    blk = WINDOW

    @pl.when(n == 0)
    def _():
        kprev[...] = jnp.zeros_like(kprev)
        vprev[...] = jnp.zeros_like(vprev)

    gmat = gmat_ref[...]
    cos2 = cos_ref[...]
    sin2 = sin_ref[...]
    q = _head_norm_rope(q_ref[...], gq_ref[...], cos2, sin2, gmat) * (HEAD_DIM ** -0.5)
    k = _head_norm_rope(k_ref[...], gk_ref[...], cos2, sin2, gmat)
    v = v_ref[...]
    kc = jnp.concatenate([kprev[...], k], axis=0).astype(BF16)
    vc = jnp.concatenate([vprev[...], v], axis=0).astype(BF16)
    qb = q.astype(BF16)

    rows = GQA * blk
    i = lax.broadcasted_iota(jnp.int32, (rows, 2 * blk), 0) & (blk - 1)
    j = lax.broadcasted_iota(jnp.int32, (rows, 2 * blk), 1)
    mask = (j > i) & (j <= i + blk) & ((n > 0) | (j >= blk))

    pieces = []
    for g in range(N_KV_HEADS):
        qg = jnp.concatenate(
            [qb[:, (g * GQA + h) * HEAD_DIM:(g * GQA + h + 1) * HEAD_DIM] for h in range(GQA)], axis=0)
        kg = kc[:, g * HEAD_DIM:(g + 1) * HEAD_DIM]
        vg = vc[:, g * HEAD_DIM:(g + 1) * HEAD_DIM]
        lg = lax.dot_general(qg, kg, (((1,), (1,)), ((), ())), preferred_element_type=F32)
        p = _sink_softmax(lg, mask, sink_ref[g])
        og = jnp.dot(p.astype(BF16), vg, preferred_element_type=F32)
        pieces.extend(og[h * blk:(h + 1) * blk] for h in range(GQA))
    ao = jnp.concatenate(pieces, axis=1)
    ao_ref[...] = _rms_rows(ao, gao_ref[...]).astype(BF16)

    kprev[...] = k
    vprev[...] = v

    @pl.when(n == nb - 1)
    def _():
        klast_ref[0] = k


def _attn_prompt(q, k, v, cos2, sin2, gq, gk, sink_col, gao, gmat, batch, seq):
    nb = seq // WINDOW
    row = lambda b, n: (b * nb + n, 0)
    const2 = lambda b, n: (0, 0)
    return pl.pallas_call(
        functools.partial(_attn_prompt_body, nb=nb),
        grid=(batch, nb),
        in_specs=[
            pl.BlockSpec((WINDOW, ATTN_WIDTH), row),
            pl.BlockSpec((WINDOW, KV_WIDTH), row),
            pl.BlockSpec((WINDOW, KV_WIDTH), row),
            pl.BlockSpec((WINDOW, LANES), lambda b, n: (n, 0)),
            pl.BlockSpec((WINDOW, LANES), lambda b, n: (n, 0)),
            pl.BlockSpec((1, ATTN_WIDTH), const2),
            pl.BlockSpec((1, KV_WIDTH), const2),
            pl.BlockSpec((N_KV_HEADS, GQA * WINDOW, 1), lambda b, n: (0, 0, 0)),
            pl.BlockSpec((1, ATTN_WIDTH), const2),
            pl.BlockSpec((MXU_DIM, MXU_DIM), const2),
        ],
        out_specs=[
            pl.BlockSpec((WINDOW, ATTN_WIDTH), row),
            pl.BlockSpec((1, WINDOW, KV_WIDTH), lambda b, n: (b, 0, 0)),
        ],
        out_shape=[
            jax.ShapeDtypeStruct((batch * seq, ATTN_WIDTH), BF16),
            jax.ShapeDtypeStruct((batch, WINDOW, KV_WIDTH), F32),
        ],
        scratch_shapes=[pltpu.VMEM((WINDOW, KV_WIDTH), F32), pltpu.VMEM((WINDOW, KV_WIDTH), F32)],
        compiler_params=_cparams(("arbitrary", "arbitrary")),
        name="attn_prompt",
    )(q, k, v, cos2, sin2, gq, gk, sink_col, gao, gmat)


def _attn_sample_body(q_ref, k_ref, v_ref, ck_ref, cv_ref, cos_ref, sin_ref, gq_ref, gk_ref, sink_ref, gao_ref,
                      gmat_ref, ao_ref, ks_ref, vs_ref, *, bb, s):
    gmat = gmat_ref[...]
    cos2 = cos_ref[...]
    sin2 = sin_ref[...]
    q = _head_norm_rope(q_ref[...], gq_ref[...], cos2, sin2, gmat) * (HEAD_DIM ** -0.5)
    k = _head_norm_rope(k_ref[...], gk_ref[...], cos2, sin2, gmat)
    v = v_ref[...]
    ck = ck_ref[...]
    cv = cv_ref[...]
    k3 = k.reshape(bb, s, KV_WIDTH)
    v3 = v.reshape(bb, s, KV_WIDTH)
    ks_ref[:, :WINDOW - s, :] = ck[:, s:, :]
    ks_ref[:, WINDOW - s:, :] = k3
    vs_ref[:, :WINDOW - s, :] = cv[:, s:, :]
    vs_ref[:, WINDOW - s:, :] = v3

    q3 = q.astype(BF16).reshape(bb, s, ATTN_WIDTH)
    ckb = ck.astype(BF16)
    cvb = cv.astype(BF16)
    k3b = k3.astype(BF16)
    v3b = v3.astype(BF16)
    rows = GQA * s
    qi = lax.broadcasted_iota(jnp.int32, (bb, rows, WINDOW), 1) & (s - 1)
    tc = lax.broadcasted_iota(jnp.int32, (bb, rows, WINDOW), 2)
    mask_c = tc > qi
    qi2 = lax.broadcasted_iota(jnp.int32, (bb, rows, s), 1) & (s - 1)
    tn = lax.broadcasted_iota(jnp.int32, (bb, rows, s), 2)
    mask_n = tn <= qi2

    pieces = []
    for g in range(N_KV_HEADS):
        qg = jnp.concatenate(
            [q3[:, :, (g * GQA + h) * HEAD_DIM:(g * GQA + h + 1) * HEAD_DIM] for h in range(GQA)], axis=1)
        sl = slice(g * HEAD_DIM, (g + 1) * HEAD_DIM)
        lc = jnp.einsum("bqd,bkd->bqk", qg, ckb[:, :, sl], preferred_element_type=F32)
        ln = jnp.einsum("bqd,bkd->bqk", qg, k3b[:, :, sl], preferred_element_type=F32)
        lc = jnp.where(mask_c, lc, NEG_INF)
        ln = jnp.where(mask_n, ln, NEG_INF)
        sink = sink_ref[g]
        m = jnp.maximum(jnp.maximum(jnp.max(lc, axis=-1, keepdims=True), jnp.max(ln, axis=-1, keepdims=True)), sink)
        ec = jnp.exp(lc - m)
        en = jnp.exp(ln - m)
        den = jnp.sum(ec, axis=-1, keepdims=True) + jnp.sum(en, axis=-1, keepdims=True) + jnp.exp(sink - m)
        og = (jnp.einsum("bqk,bkd->bqd", (ec / den).astype(BF16), cvb[:, :, sl], preferred_element_type=F32)
              + jnp.einsum("bqk,bkd->bqd", (en / den).astype(BF16), v3b[:, :, sl], preferred_element_type=F32))
        pieces.extend(og[:, h * s:(h + 1) * s, :] for h in range(GQA))
    ao = jnp.concatenate(pieces, axis=2).reshape(bb * s, ATTN_WIDTH)
    ao_ref[...] = _rms_rows(ao, gao_ref[...]).astype(BF16)


def _attn_sample(q, k, v, ck, cv, cos2, sin2, gq, gk, sink_col, gao, gmat, batch, s, bb):
    rows = bb * s
    row = lambda i: (i, 0)
    const2 = lambda i: (0, 0)
    b3 = lambda i: (i, 0, 0)
    return pl.pallas_call(
        functools.partial(_attn_sample_body, bb=bb, s=s),
        grid=(batch // bb,),
        in_specs=[
            pl.BlockSpec((rows, ATTN_WIDTH), row),
            pl.BlockSpec((rows, KV_WIDTH), row),
            pl.BlockSpec((rows, KV_WIDTH), row),
            pl.BlockSpec((bb, WINDOW, KV_WIDTH), b3),
            pl.BlockSpec((bb, WINDOW, KV_WIDTH), b3),
            pl.BlockSpec((rows, LANES), const2),
            pl.BlockSpec((rows, LANES), const2),
            pl.BlockSpec((1, ATTN_WIDTH), const2),
            pl.BlockSpec((1, KV_WIDTH), const2),
            pl.BlockSpec((N_KV_HEADS, GQA * s, 1), lambda i: (0, 0, 0)),
            pl.BlockSpec((1, ATTN_WIDTH), const2),
            pl.BlockSpec((MXU_DIM, MXU_DIM), const2),
        ],
        out_specs=[
            pl.BlockSpec((rows, ATTN_WIDTH), row),
            pl.BlockSpec((bb, WINDOW, KV_WIDTH), b3),
            pl.BlockSpec((bb, WINDOW, KV_WIDTH), b3),
        ],
        out_shape=[
            jax.ShapeDtypeStruct((batch * s, ATTN_WIDTH), BF16),
            jax.ShapeDtypeStruct((batch, WINDOW, KV_WIDTH), F32),
            jax.ShapeDtypeStruct((batch, WINDOW, KV_WIDTH), F32),
        ],
        compiler_params=_cparams(("arbitrary",)),
        name="attn_sample",
    )(q, k, v, ck, cv, cos2, sin2, gq, gk, sink_col, gao, gmat)


def _lru_gates(xc, wa_ref, wx_ref, ba, bx, lam):
    xb = xc.astype(BF16)
    r_cols, i_cols = [], []
    for c in range(LRU_WIDTH // MXU_DIM):
        sl = slice(c * MXU_DIM, (c + 1) * MXU_DIM)
        r_cols.append(jnp.dot(xb[:, sl], wa_ref[c], preferred_element_type=F32))
        i_cols.append(jnp.dot(xb[:, sl], wx_ref[c], preferred_element_type=F32))
    r = 1.0 / (1.0 + jnp.exp(-(jnp.concatenate(r_cols, axis=1) + ba)))
    ig = 1.0 / (1.0 + jnp.exp(-(jnp.concatenate(i_cols, axis=1) + bx)))
    neg_lam = -lam
    softplus = jnp.maximum(neg_lam, 0.0) + jnp.log1p(jnp.exp(-jnp.abs(neg_lam)))
    log_a = -LRU_C * r * softplus
    a = jnp.exp(log_a)
    u = jnp.sqrt(-jnp.tanh(log_a) * (a * a + 1.0)) * (ig * xc)
    return a, u


def _segment_scan(a, u, seg):
    t = lax.broadcasted_iota(jnp.int32, a.shape, 0) & (seg - 1)
    d = 1
    while d < seg:
        valid = t >= d
        u = jnp.where(valid, a * pltpu.roll(u, d, axis=0) + u, u)
        a = jnp.where(valid, a * pltpu.roll(a, d, axis=0), a)
        d *= 2
    return a, u


def _lru_prompt_body(xr_ref, yg_ref, cw_ref, cb_ref, wa_ref, wx_ref, ba_ref, bx_ref, lam_ref, glo_ref,
                     lo_ref, hlast_ref, prev8, hcarry, *, tb):
    t_blk = pl.program_id(1)

    @pl.when(t_blk == 0)
    def _():
        prev8[...] = jnp.zeros_like(prev8)
        hcarry[...] = jnp.zeros_like(hcarry)

    xr = xr_ref[...]
    p8 = prev8[...]
    row8 = lax.broadcasted_iota(jnp.int32, p8.shape, 0)
    xc = xr * cw_ref[CONV_W - 1:CONV_W, :] + cb_ref[...]
    for kshift in range(1, CONV_W):
        rolled = pltpu.roll(xr, kshift, axis=0)
        top = jnp.where(row8 < kshift, pltpu.roll(p8, kshift, axis=0), rolled[:SUBLANES])
        shifted = jnp.concatenate([top, rolled[SUBLANES:]], axis=0)
        xc = xc + shifted * cw_ref[CONV_W - 1 - kshift:CONV_W - kshift, :]
    a, u = _lru_gates(xc, wa_ref, wx_ref, ba_ref[...], bx_ref[...], lam_ref[...])
    acum, hloc = _segment_scan(a, u, tb)
    h = hloc + acum * hcarry[...]
    y = h * jax.nn.gelu(yg_ref[...])
    lo_ref[...] = _rms_rows(y, glo_ref[...]).astype(BF16)
    hlast = h[tb - 1:tb, :]
    hcarry[...] = hlast
    prev8[...] = xr[tb - SUBLANES:, :]
    hlast_ref[0] = hlast


def _lru_prompt(xr, yg, cw, cb, wa_bd, wx_bd, ba, bx, lam, glo, batch, seq, tb):
    nt = seq // tb
    row = lambda b, t: (b * nt + t, 0)
    const2 = lambda b, t: (0, 0)
    const3 = lambda b, t: (0, 0, 0)
    nchunk = LRU_WIDTH // MXU_DIM
    return pl.pallas_call(
        functools.partial(_lru_prompt_body, tb=tb),
        grid=(batch, nt),
        in_specs=[
            pl.BlockSpec((tb, LRU_WIDTH), row),
            pl.BlockSpec((tb, LRU_WIDTH), row),
            pl.BlockSpec((CONV_W, LRU_WIDTH), const2),
            pl.BlockSpec((1, LRU_WIDTH), const2),
            pl.BlockSpec((nchunk, MXU_DIM, MXU_DIM), const3),
            pl.BlockSpec((nchunk, MXU_DIM, MXU_DIM), const3),
            pl.BlockSpec((1, LRU_WIDTH), const2),
            pl.BlockSpec((1, LRU_WIDTH), const2),
            pl.BlockSpec((1, LRU_WIDTH), const2),
            pl.BlockSpec((1, LRU_WIDTH), const2),
        ],
        out_specs=[
            pl.BlockSpec((tb, LRU_WIDTH), row),
            pl.BlockSpec((1, 1, LRU_WIDTH), lambda b, t: (b, 0, 0)),
        ],
        out_shape=[
            jax.ShapeDtypeStruct((batch * seq, LRU_WIDTH), BF16),
            jax.ShapeDtypeStruct((batch, 1, LRU_WIDTH), F32),
        ],
        scratch_shapes=[pltpu.VMEM((SUBLANES, LRU_WIDTH), F32), pltpu.VMEM((1, LRU_WIDTH), F32)],
        compiler_params=_cparams(("arbitrary", "arbitrary")),
        name="lru_prompt",
    )(xr, yg, cw, cb, wa_bd, wx_bd, ba, bx, lam, glo)


def _lru_sample_body(xr_ref, yg_ref, cpad_ref, h0_ref, cw_ref, cb_ref, wa_ref, wx_ref, ba_ref, bx_ref, lam_ref,
                     glo_ref, lo_ref, h_ref, *, s):
    xr = xr_ref[...]
    cpad = cpad_ref[...]
    rows = xr.shape[0]
    t = lax.broadcasted_iota(jnp.int32, xr.shape, 0) & (s - 1)
    xc = xr * cw_ref[CONV_W - 1:CONV_W, :] + cb_ref[...]
    for kshift in range(1, CONV_W):
        shifted = jnp.where(t < kshift, pltpu.roll(cpad, rows - s + kshift, axis=0), pltpu.roll(xr, kshift, axis=0))
        xc = xc + shifted * cw_ref[CONV_W - 1 - kshift:CONV_W - kshift, :]
    a, u = _lru_gates(xc, wa_ref, wx_ref, ba_ref[...], bx_ref[...], lam_ref[...])
    acum, hloc = _segment_scan(a, u, s)
    h = hloc + acum * h0_ref[...]
    h_ref[...] = h
    y = h * jax.nn.gelu(yg_ref[...])
    lo_ref[...] = _rms_rows(y, glo_ref[...]).astype(BF16)


def _lru_sample(xr, yg, cpad, h0rows, cw, cb, wa_bd, wx_bd, ba, bx, lam, glo, s, tb):
    n = xr.shape[0]
    row = lambda i: (i, 0)
    const2 = lambda i: (0, 0)
    const3 = lambda i: (0, 0, 0)
    nchunk = LRU_WIDTH // MXU_DIM
    return pl.pallas_call(
        functools.partial(_lru_sample_body, s=s),
        grid=(n // tb,),
        in_specs=[
            pl.BlockSpec((tb, LRU_WIDTH), row),
            pl.BlockSpec((tb, LRU_WIDTH), row),
            pl.BlockSpec((tb, LRU_WIDTH), row),
            pl.BlockSpec((tb, LRU_WIDTH), row),
            pl.BlockSpec((CONV_W, LRU_WIDTH), const2),
            pl.BlockSpec((1, LRU_WIDTH), const2),
            pl.BlockSpec((nchunk, MXU_DIM, MXU_DIM), const3),
            pl.BlockSpec((nchunk, MXU_DIM, MXU_DIM), const3),
            pl.BlockSpec((1, LRU_WIDTH), const2),
            pl.BlockSpec((1, LRU_WIDTH), const2),
            pl.BlockSpec((1, LRU_WIDTH), const2),
            pl.BlockSpec((1, LRU_WIDTH), const2),
        ],
        out_specs=[pl.BlockSpec((tb, LRU_WIDTH), row), pl.BlockSpec((tb, LRU_WIDTH), row)],
        out_shape=[jax.ShapeDtypeStruct((n, LRU_WIDTH), BF16), jax.ShapeDtypeStruct((n, LRU_WIDTH), F32)],
        compiler_params=_cparams(("arbitrary",)),
        name="lru_sample",
    )(xr, yg, cpad, h0rows, cw, cb, wa_bd, wx_bd, ba, bx, lam, glo)


def _outproj_body(x_ref, ao_ref, lo_ref, w_ref, g_ref, x1_ref, hnt_ref):
    mixed = (jnp.dot(ao_ref[...], w_ref[:ATTN_WIDTH, :], preferred_element_type=F32)
             + jnp.dot(lo_ref[...], w_ref[ATTN_WIDTH:, :], preferred_element_type=F32))
    x1 = x_ref[...] + mixed
    x1_ref[...] = x1
    hnt_ref[...] = _rms_rows(x1, g_ref[...]).T.astype(BF16)


def _outproj(x, ao, lo, w_bf, g, tm):
    n = x.shape[0]
    row = lambda i: (i, 0)
    const2 = lambda i: (0, 0)
    return pl.pallas_call(
        _outproj_body,
        grid=(n // tm,),
        in_specs=[
            pl.BlockSpec((tm, D_MODEL), row),
            pl.BlockSpec((tm, ATTN_WIDTH), row),
            pl.BlockSpec((tm, LRU_WIDTH), row),
            pl.BlockSpec((D_MODEL, D_MODEL), const2),
            pl.BlockSpec((1, D_MODEL), const2),
        ],
        out_specs=[pl.BlockSpec((tm, D_MODEL), row), pl.BlockSpec((D_MODEL, tm), lambda i: (0, i))],
        out_shape=[jax.ShapeDtypeStruct((n, D_MODEL), F32), jax.ShapeDtypeStruct((D_MODEL, n), BF16)],
        compiler_params=_cparams(("arbitrary",)),
        name="outproj",
    )(x, ao, lo, w_bf, g)


PEER_CAND_COLS = tuple(min(PEER_TOPK, (PEER_TOPK + 1) // (a + 1)) for a in range(PEER_TOPK))
ROUTE_LANES = 128
ROUTE_TILES_PER_STEP = 2


def _sort_network(n):
    pairs, p = [], 1
    while p < n:
        k = p
        while k >= 1:
            for j in range(k % p, n - k, 2 * k):
                for i in range(min(k, n - j - k)):
                    if (i + j) // (2 * p) == (i + j + k) // (2 * p):
                        pairs.append((i + j, i + j + k))
            k //= 2
        p *= 2
    return tuple(pairs)


SORT16 = _sort_network(N_KEYS // SUBLANES)


def _pop_lists(heads_and_ids, nvals):
    levels = [list(lv) for lv, _ in heads_and_ids]
    ids = [i for _, i in heads_and_ids]
    big = float(SUBLANES * len(levels))
    vals = []
    for r in range(nvals):
        top = levels[0][0]
        for lv in levels[1:]:
            top = jnp.maximum(top, lv[0])
        m = jnp.max(top, axis=0, keepdims=True)
        vals.append(m)
        if r == nvals - 1:
            break
        first = None
        for lv, i in zip(levels, ids):
            cand = jnp.where(lv[0] == m, i, big)
            first = cand if first is None else jnp.minimum(first, cand)
        first = jnp.min(first, axis=0, keepdims=True)
        for lv, i in zip(levels, ids):
            hit = i == first
            live = min(len(lv), nvals - 1 - r)
            for d in range(live):
                below = lv[d + 1] if d + 1 < len(lv) else -jnp.inf
                lv[d] = jnp.where(hit, below, lv[d])
    return vals


def _top16(s, ltri, ones8):
    lanes = s.shape[1]
    lv = [s[SUBLANES * r:SUBLANES * (r + 1)] for r in range(N_KEYS // SUBLANES)]
    for i, j in SORT16:
        lv[i], lv[j] = jnp.maximum(lv[i], lv[j]), jnp.minimum(lv[i], lv[j])
    subf = lax.broadcasted_iota(jnp.int32, (SUBLANES, lanes), 0).astype(F32)
    vals = _pop_lists([(lv, subf)], PEER_TOPK)
    v16 = vals[PEER_TOPK - 1]
    gt = s > v16
    eq = s == v16
    n_gt = jnp.dot(ones8, jnp.where(gt, 1.0, 0.0).astype(BF16), preferred_element_type=F32)[0:1]
    eq_before = jnp.dot(ltri, jnp.where(eq, 1.0, 0.0).astype(BF16), preferred_element_type=F32)
    need = float(PEER_TOPK) - n_gt
    kept = jnp.where(gt, s, jnp.where(eq, jnp.where(eq_before < need, s, -jnp.inf), -jnp.inf))
    return vals, kept


def _best_sums(v0, v1):
    lanes = v0[0].shape[1]
    lo0 = jnp.concatenate(v0[:SUBLANES], axis=0)
    hi0 = jnp.concatenate(v0[SUBLANES:], axis=0)
    row8 = lax.broadcasted_iota(jnp.int32, (SUBLANES, lanes), 0)
    subf = row8.astype(F32)
    levels = []
    for b in range(PEER_TOPK):
        c = lo0 + v1[b]
        nrows = sum(1 for a in range(SUBLANES) if PEER_CAND_COLS[a] > b)
        levels.append(c if nrows == SUBLANES else jnp.where(row8 < nrows, c, -jnp.inf))
    return _pop_lists([(levels, subf), ([hi0 + v1[0]], subf + float(SUBLANES))], PEER_TOPK + 1)


def _route_body(hnt_ref, wq_ref, keys_ref, ltri_ref, ones_ref, s1m_ref, e1z_ref, t0_ref, e0_ref, s_scr, *, tm):
    qt = jnp.dot(wq_ref[...], hnt_ref[...], preferred_element_type=F32)
    half = D_KEY // 2
    for hc in range(2 * PEER_HEADS):
        qhc = qt[hc * half:(hc + 1) * half, :].astype(BF16)
        s_scr[hc] = jnp.dot(keys_ref[hc], qhc, preferred_element_type=F32)

    def route_tile(h, lanes):
        ltri = ltri_ref[...]
        ones8 = ones_ref[...]
        v0, s0m = _top16(s_scr[2 * h, :, lanes], ltri, ones8)
        v1, s1m = _top16(s_scr[2 * h + 1, :, lanes], ltri, ones8)
        best = _best_sums(v0, v1)
        mx = best[0]
        z = jnp.exp(best[0] - mx)
        for r in range(1, PEER_TOPK):
            z = z + jnp.exp(best[r] - mx)
        thr = 0.5 * (best[PEER_TOPK - 1] + best[PEER_TOPK])
        s1m_ref[h, :, lanes] = s1m
        e1z_ref[h, :, lanes] = jnp.exp(s1m - v1[0]) / z
        t0_ref[h, :, lanes] = thr - s0m
        e0_ref[h, :, lanes] = jnp.exp(s0m - v0[0])

    per_head = tm // (ROUTE_TILES_PER_STEP * ROUTE_LANES)

    def step(it, carry):
        h = it // per_head
        c = it % per_head
        for t in range(ROUTE_TILES_PER_STEP):
            start = (c * ROUTE_TILES_PER_STEP + t) * ROUTE_LANES
            route_tile(h, pl.ds(pl.multiple_of(start, ROUTE_LANES), ROUTE_LANES))
        return carry

    lax.fori_loop(0, PEER_HEADS * per_head, step, 0)


def _route(hnt, wq_t, keys, tm):
    n = hnt.shape[1]
    blk = lambda i: (0, 0, i)
    shape = jax.ShapeDtypeStruct((PEER_HEADS, N_KEYS, n), F32)
    spec = pl.BlockSpec((PEER_HEADS, N_KEYS, tm), blk)
    ltri = jnp.asarray(np.tril(np.ones((N_KEYS, N_KEYS)), -1), BF16)
    ones8 = jnp.ones((SUBLANES, N_KEYS), BF16)
    return pl.pallas_call(
        functools.partial(_route_body, tm=tm),
        grid=(n // tm,),
        in_specs=[
            pl.BlockSpec((D_MODEL, tm), lambda i: (0, i)),
            pl.BlockSpec((PEER_HEADS * D_KEY, D_MODEL), lambda i: (0, 0)),
            pl.BlockSpec((2 * PEER_HEADS, N_KEYS, D_KEY // 2), lambda i: (0, 0, 0)),
            pl.BlockSpec((N_KEYS, N_KEYS), lambda i: (0, 0)),
            pl.BlockSpec((SUBLANES, N_KEYS), lambda i: (0, 0)),
        ],
        out_specs=[spec, spec, spec, spec],
        out_shape=[shape, shape, shape, shape],
        scratch_shapes=[pltpu.VMEM((2 * PEER_HEADS, N_KEYS, tm), F32)],
        compiler_params=_cparams(("arbitrary",)),
        name="peer_route",
    )(hnt, wq_t, keys, ltri, ones8)


EXPERT_LANES = 256
EXPERT_BLOCK = 512
EXPERT_ROWS = EXPERT_BLOCK // N_KEYS
U_KCHUNK = 512
U_NCHUNK = D_MODEL // U_KCHUNK
ROWS_WITH_DOT1 = 3


def _routing_weight(s1m_ref, e1z_ref, t0_ref, e0_ref, k, ii, lanes):
    width = lanes.stop - lanes.start
    trow = [jnp.broadcast_to(t0_ref[h, k, ii:ii + 1, lanes], (SUBLANES, width)) for h in range(PEER_HEADS)]
    e0row = [jnp.broadcast_to(e0_ref[h, k, ii:ii + 1, lanes], (SUBLANES, width)) for h in range(PEER_HEADS)]
    out = []
    for j in range(0, N_KEYS, SUBLANES):
        w = None
        for h in range(PEER_HEADS):
            term = jnp.where(s1m_ref[h, j:j + SUBLANES, lanes] >= trow[h], e1z_ref[h, j:j + SUBLANES, lanes], 0.0) * e0row[h]
            w = term if w is None else w + term
        out.append(w)
    return jnp.concatenate(out, axis=0)


def _experts_body(hnt_ref, *refs, tm, nk):
    u_refs = refs[:2 * U_NCHUNK]
    (vtp_ref, vta_ref, s1m_ref, e1z_ref, t0_ref, e0_ref, x1_ref, y_ref,
     acc, act, wbuf, pt_a, pt_b) = refs[2 * U_NCHUNK:]
    k = pl.program_id(1)
    kt = jnp.minimum(k, nk - 1)
    nlane = tm // EXPERT_LANES

    def weights(blk, row_lo, row_hi):
        for ii in range(blk * EXPERT_ROWS + row_lo, blk * EXPERT_ROWS + row_hi):
            rows = slice(ii * N_KEYS, (ii + 1) * N_KEYS)
            for c in range(nlane):
                lanes = slice(c * EXPERT_LANES, (c + 1) * EXPERT_LANES)
                wbuf[rows, lanes] = _routing_weight(s1m_ref, e1z_ref, t0_ref, e0_ref, kt, ii, lanes)

    def weights_tile(ii, c):
        rows = slice(ii * N_KEYS, (ii + 1) * N_KEYS)
        lanes = slice(c * EXPERT_LANES, (c + 1) * EXPERT_LANES)
        wbuf[rows, lanes] = _routing_weight(s1m_ref, e1z_ref, t0_ref, e0_ref, kt, ii, lanes)

    def dot1_piece(blk, c):
        lanes = slice(c * EXPERT_LANES, (c + 1) * EXPERT_LANES)
        a = None
        for kc in range(U_NCHUNK):
            u_ref = u_refs[blk * U_NCHUNK + kc]
            d = jnp.dot(u_ref[...], hnt_ref[kc * U_KCHUNK:(kc + 1) * U_KCHUNK, lanes], preferred_element_type=F32)
            a = d if a is None else a + d
        act[:, lanes] = a

    def dot1(blk):
        for c in range(nlane):
            for ii in range(blk * EXPERT_ROWS, blk * EXPERT_ROWS + ROWS_WITH_DOT1):
                weights_tile(ii, c)
            dot1_piece(blk, c)

    def weighted_act(blk, pt):
        for ii in range(EXPERT_ROWS):
            rows = slice(ii * N_KEYS, (ii + 1) * N_KEYS)
            wrows = slice((blk * EXPERT_ROWS + ii) * N_KEYS, (blk * EXPERT_ROWS + ii + 1) * N_KEYS)
            for c in range(nlane):
                lanes = slice(c * EXPERT_LANES, (c + 1) * EXPERT_LANES)
                pt[rows, lanes] = (wbuf[wrows, lanes] * jax.nn.gelu(act[rows, lanes])).astype(BF16)

    def dot2(vt_ref, pt):
        acc[...] += jnp.dot(vt_ref[...], pt[...], preferred_element_type=F32)

    @pl.when(k == 0)
    def _():
        acc[...] = jnp.zeros_like(acc)
        pt_b[...] = jnp.zeros_like(pt_b)

    @pl.when(k < nk)
    def _():
        dot1(0)

    weights(0, ROWS_WITH_DOT1, EXPERT_ROWS)
    weighted_act(0, pt_a)
    dot2(vtp_ref, pt_b)

    @pl.when(k < nk)
    def _():
        dot1(1)

    @pl.when(kt == k)
    def _():
        weights(1, ROWS_WITH_DOT1, EXPERT_ROWS)
        weighted_act(1, pt_b)
        dot2(vta_ref, pt_a)

    @pl.when(k == nk)
    def _():
        y_ref[...] = x1_ref[...] + acc[...].T


def _experts(hnt, u_bf, vt_bf, s1m, e1z, t0, e0, x1, tm):
    n = hnt.shape[1]
    assert 2 * EXPERT_BLOCK == SUBLANES * N_KEYS
    nk = N_EXPERTS // (2 * EXPERT_BLOCK)
    nblk = N_EXPERTS // EXPERT_BLOCK
    tok = lambda t, k: (t, 0)
    once = pl.Buffered(1)
    small = pl.BlockSpec((PEER_HEADS, N_KEYS, tm), lambda t, k: (0, 0, t), pipeline_mode=once)
    tiled = pl.BlockSpec((PEER_HEADS, N_KEYS // SUBLANES, SUBLANES, tm), lambda t, k: (0, 0, 0, t), pipeline_mode=once)
    t0 = t0.reshape(PEER_HEADS, N_KEYS // SUBLANES, SUBLANES, n)
    e0 = e0.reshape(PEER_HEADS, N_KEYS // SUBLANES, SUBLANES, n)
    u_specs = [pl.BlockSpec((EXPERT_BLOCK, U_KCHUNK), lambda t, k, blk=blk, kc=kc: (2 * jnp.minimum(k, nk - 1) + blk, kc))
               for blk in range(2) for kc in range(U_NCHUNK)]
    return pl.pallas_call(
        functools.partial(_experts_body, tm=tm, nk=nk),
        grid=(n // tm, nk + 1),
        in_specs=[
            pl.BlockSpec((D_MODEL, tm), lambda t, k: (0, t), pipeline_mode=once),
            *u_specs,
            pl.BlockSpec((D_MODEL, EXPERT_BLOCK), lambda t, k: (0, jnp.maximum(2 * k - 1, 0))),
            pl.BlockSpec((D_MODEL, EXPERT_BLOCK), lambda t, k: (0, jnp.minimum(2 * k, nblk - 1))),
            small, small, tiled, tiled,
            pl.BlockSpec((tm, D_MODEL), tok, pipeline_mode=once),
        ],
        out_specs=pl.BlockSpec((tm, D_MODEL), tok),
        out_shape=jax.ShapeDtypeStruct((n, D_MODEL), F32),
        scratch_shapes=[
            pltpu.VMEM((D_MODEL, tm), F32),
            pltpu.VMEM((EXPERT_BLOCK, tm), F32),
            pltpu.VMEM((2 * EXPERT_BLOCK, tm), F32),
            pltpu.VMEM((EXPERT_BLOCK, tm), BF16),
            pltpu.VMEM((EXPERT_BLOCK, tm), BF16),
        ],
        compiler_params=_cparams(("arbitrary", "arbitrary")),
        name="peer_experts",
    )(hnt, *([u_bf] * len(u_specs)), vt_bf, vt_bf, s1m, e1z, t0, e0, x1)


def _rope_tables(pos):
    half = HEAD_DIM // 2
    inv = jnp.exp(-math.log(ROPE_THETA) * jnp.arange(half, dtype=F32) * (2.0 / HEAD_DIM))
    ang = pos[:, None] * inv[None, :]
    cos, sin = jnp.cos(ang), jnp.sin(ang)
    cos2 = jnp.concatenate([cos, cos, cos, cos], axis=1)
    sin2 = jnp.concatenate([-sin, sin, -sin, sin], axis=1)
    return cos2, sin2


def _block_diag(w):
    per = MXU_DIM // LRU_BLOCK
    w4 = w.reshape(LRU_WIDTH // MXU_DIM, per, LRU_BLOCK, LRU_BLOCK)
    eye = jnp.eye(per, dtype=w.dtype)
    return jnp.einsum("cpde,pq->cpdqe", w4, eye).reshape(LRU_WIDTH // MXU_DIM, MXU_DIM, MXU_DIM).astype(BF16)


def kernel(x_prompt, x_sample, cache_k, cache_v, state_conv, state_h, norm_mix_g, w_in, q_norm_g, k_norm_g, attn_sinks, conv_w, conv_b, w_rec_gate, b_rec_gate, w_in_gate, b_in_gate, lru_lambda, attn_out_g, lru_out_g, w_out, norm_ffn_g, w_peer_q, peer_sub_keys, peer_u, peer_v):
    depth = w_in.shape[0]
    assert depth == 1
    batch, seq, _ = x_prompt.shape
    dbatch, dseq, _ = x_sample.shape
    past_len = 8192
    l = 0

    w_in_bf = w_in[l].astype(BF16)
    w_out_bf = w_out[l].astype(BF16)
    wq_t = w_peer_q[l].T.astype(BF16)
    keys = peer_sub_keys[l].reshape(2 * PEER_HEADS, N_KEYS, D_KEY // 2).astype(BF16)
    u_bf = peer_u[l].astype(BF16)
    vt_bf = peer_v[l].T.astype(BF16)
    wa_bd = _block_diag(w_rec_gate[l])
    wx_bd = _block_diag(w_in_gate[l])
    ba = b_rec_gate[l].reshape(1, LRU_WIDTH)
    bx = b_in_gate[l].reshape(1, LRU_WIDTH)
    lam = lru_lambda[l].reshape(1, LRU_WIDTH)
    cw = conv_w[l]
    cb = conv_b[l].reshape(1, LRU_WIDTH)
    g_mix = norm_mix_g[l].reshape(1, D_MODEL)
    g_ffn = norm_ffn_g[l].reshape(1, D_MODEL)
    gq = jnp.tile(q_norm_g[l], N_Q_HEADS).reshape(1, ATTN_WIDTH)
    gk = jnp.tile(k_norm_g[l], N_KV_HEADS).reshape(1, KV_WIDTH)
    gao = attn_out_g[l].reshape(1, ATTN_WIDTH)
    glo = lru_out_g[l].reshape(1, LRU_WIDTH)
    gmat = jnp.asarray(np.kron(np.eye(MXU_DIM // HEAD_DIM), np.full((HEAD_DIM, HEAD_DIM), 1.0 / HEAD_DIM)), BF16)
    sinks = attn_sinks[l].astype(F32).reshape(N_KV_HEADS, GQA)
    sink_p = jnp.repeat(sinks, WINDOW, axis=1).reshape(N_KV_HEADS, GQA * WINDOW, 1)
    sink_s = jnp.repeat(sinks, dseq, axis=1).reshape(N_KV_HEADS, GQA * dseq, 1)
    cos_p, sin_p = _rope_tables(jnp.arange(seq, dtype=F32))
    cos_s, sin_s = _rope_tables(float(past_len) + jnp.arange(dseq, dtype=F32))

    n_p = batch * seq
    n_s = dbatch * dseq
    xp2 = x_prompt.reshape(n_p, D_MODEL)
    xs2 = x_sample.reshape(n_s, D_MODEL)

    qp, kp, vp, xrp, ygp = _inproj(xp2, g_mix, w_in_bf, 256)
    ao_p, k_last = _attn_prompt(qp, kp, vp, cos_p, sin_p, gq, gk, sink_p, gao, gmat, batch, seq)
    lo_p, h_p = _lru_prompt(xrp, ygp, cw, cb, wa_bd, wx_bd, ba, bx, lam, glo, batch, seq, 256)
    x1p, hn_p = _outproj(xp2, ao_p, lo_p, w_out_bf, g_ffn, 256)

    bb = 8
    qs, ks, vs, xrs, ygs = _inproj(xs2, g_mix, w_in_bf, 256)
    cos_sb = jnp.tile(cos_s, (bb, 1))
    sin_sb = jnp.tile(sin_s, (bb, 1))
    ck = cache_k[l].reshape(dbatch, WINDOW, KV_WIDTH)
    cv = cache_v[l].reshape(dbatch, WINDOW, KV_WIDTH)
    ao_s, k_s, v_s = _attn_sample(qs, ks, vs, ck, cv, cos_sb, sin_sb, gq, gk, sink_s, gao, gmat, dbatch, dseq, bb)
    cpad = jnp.pad(state_conv[l], ((0, 0), (dseq - (CONV_W - 1), 0), (0, 0))).reshape(n_s, LRU_WIDTH)
    h0rows = jnp.repeat(state_h[l], dseq, axis=0)
    lo_s, h_s = _lru_sample(xrs, ygs, cpad, h0rows, cw, cb, wa_bd, wx_bd, ba, bx, lam, glo, dseq, 256)
    x1s, hn_s = _outproj(xs2, ao_s, lo_s, w_out_bf, g_ffn, 256)

    tm = 512
    outs = []
    for x1, hn in ((x1p, hn_p), (x1s, hn_s)):
        s1m, e1z, t0, e0 = _route(hn, wq_t, keys, tm)
        outs.append(_experts(hn, u_bf, vt_bf, s1m, e1z, t0, e0, x1, tm))
    y_p = outs[0].reshape(batch, seq, D_MODEL)
    y_s = outs[1].reshape(dbatch, dseq, D_MODEL)

    kv_shape = (N_KV_HEADS, HEAD_DIM)
    k_prompt = k_last.reshape(1, batch, WINDOW, *kv_shape)
    v_prompt = vp.reshape(batch, seq, *kv_shape)[:, -WINDOW:][None]
    conv_prompt = xrp.reshape(batch, seq, LRU_WIDTH)[:, -(CONV_W - 1):][None]
    h_prompt = h_p.reshape(1, batch, LRU_WIDTH)
    k_sample = k_s.reshape(1, dbatch, WINDOW, *kv_shape)
    v_sample = v_s.reshape(1, dbatch, WINDOW, *kv_shape)
    conv_sample = xrs.reshape(dbatch, dseq, LRU_WIDTH)[:, -(CONV_W - 1):][None]
    h_sample = h_s.reshape(dbatch, dseq, LRU_WIDTH)[:, -1][None]
    return (y_p, y_s, k_prompt, v_prompt, conv_prompt, h_prompt, k_sample, v_sample, conv_sample, h_sample)
```

```python
import functools
import math

import jax
import jax.numpy as jnp
import numpy as np
from jax import lax
from jax.experimental import pallas as pl
from jax.experimental.pallas import tpu as pltpu

F32 = jnp.float32
BF16 = jnp.bfloat16

D_MODEL = 2048
HEAD_DIM = 64
N_Q_HEADS = 16
N_KV_HEADS = 4
GQA = N_Q_HEADS // N_KV_HEADS
ATTN_WIDTH = N_Q_HEADS * HEAD_DIM
KV_WIDTH = N_KV_HEADS * HEAD_DIM
WINDOW = 128
ROPE_THETA = 10000.0
LRU_WIDTH = D_MODEL - ATTN_WIDTH
LRU_BLOCK = 64
CONV_W = 4
LRU_C = 8.0
N_KEYS = 128
N_EXPERTS = N_KEYS * N_KEYS
PEER_HEADS = 8
PEER_TOPK = 16
D_KEY = 256
EPS = 1e-6
NEG_INF = -1e30

MXU_DIM = 256
LANES = 128
SUBLANES = 8
VMEM_LIMIT_BYTES = 60 * 1024 * 1024


def _cparams(sem, flags=None):
    return pltpu.CompilerParams(dimension_semantics=sem, vmem_limit_bytes=VMEM_LIMIT_BYTES, flags=flags)


IN_CUTS = (0, ATTN_WIDTH, ATTN_WIDTH + KV_WIDTH, ATTN_WIDTH + 2 * KV_WIDTH,
           ATTN_WIDTH + 2 * KV_WIDTH + LRU_WIDTH, ATTN_WIDTH + 2 * KV_WIDTH + 2 * LRU_WIDTH)


def _inproj_body(x_ref, g_ref, w_ref, q_ref, k_ref, v_ref, xr_ref, yg_ref):
    x = x_ref[...]
    ms = jnp.mean(x * x, axis=-1, keepdims=True)
    hn = (x * lax.rsqrt(ms + EPS) * g_ref[...]).astype(BF16)
    for o_ref, lo, hi in zip((q_ref, k_ref, v_ref, xr_ref, yg_ref), IN_CUTS[:-1], IN_CUTS[1:]):
        o_ref[...] = jnp.dot(hn, w_ref[:, lo:hi], preferred_element_type=F32)


def _inproj(x, g, w_bf, tm):
    n = x.shape[0]
    widths = [hi - lo for lo, hi in zip(IN_CUTS[:-1], IN_CUTS[1:])]
    return pl.pallas_call(
        _inproj_body,
        grid=(n // tm,),
        in_specs=[
            pl.BlockSpec((tm, D_MODEL), lambda i: (i, 0)),
            pl.BlockSpec((1, D_MODEL), lambda i: (0, 0)),
            pl.BlockSpec((D_MODEL, IN_CUTS[-1]), lambda i: (0, 0)),
        ],
        out_specs=[pl.BlockSpec((tm, w), lambda i: (i, 0)) for w in widths],
        out_shape=[jax.ShapeDtypeStruct((n, w), F32) for w in widths],
        compiler_params=_cparams(("arbitrary",)),
        name="inproj",
    )(x, g, w_bf)


def _group_mean_sq(x, gmat):
    x2 = x * x
    hi = x2.astype(BF16)
    lo = (x2 - hi.astype(F32)).astype(BF16)
    cols = []
    for c in range(x.shape[1] // MXU_DIM):
        sl = slice(c * MXU_DIM, (c + 1) * MXU_DIM)
        cols.append(jnp.dot(hi[:, sl], gmat, preferred_element_type=F32)
                    + jnp.dot(lo[:, sl], gmat, preferred_element_type=F32))
    return cols[0] if len(cols) == 1 else jnp.concatenate(cols, axis=1)


def _tile_lanes(t, reps):
    return t if reps == 1 else jnp.concatenate([t] * reps, axis=1)


def _head_norm_rope(x, gain, cos2, sin2, gmat):
    width = x.shape[1]
    y = x * lax.rsqrt(_group_mean_sq(x, gmat) + EPS) * gain
    lane = lax.broadcasted_iota(jnp.int32, y.shape, 1)
    first_half = (lane & (HEAD_DIM // 2)) == 0
    partner = jnp.where(first_half,
                        pltpu.roll(y, width - HEAD_DIM // 2, axis=1),
                        pltpu.roll(y, HEAD_DIM // 2, axis=1))
    reps = width // LANES
    return y * _tile_lanes(cos2, reps) + partner * _tile_lanes(sin2, reps)


def _rms_rows(x, gain):
    ms = jnp.mean(x * x, axis=-1, keepdims=True)
    return x * lax.rsqrt(ms + EPS) * gain


def _attn_prompt_body(q_ref, k_ref, v_ref, cos_ref, sin_ref, gq_ref, gk_ref, sink_ref, gao_ref, gmat_ref, bias_ref,
                      ao_ref, klast_ref, kprev, vprev, lg_scr, p_scr, *, nb):
    n = pl.program_id(1)
    blk = WINDOW
    rows = GQA * blk

    @pl.when(n == 0)
    def _():
        kprev[...] = jnp.zeros_like(kprev)
        vprev[...] = jnp.zeros_like(vprev)

    gmat = gmat_ref[...]
    cos2 = cos_ref[...]
    sin2 = sin_ref[...]
    q = _head_norm_rope(q_ref[...], gq_ref[...], cos2, sin2, gmat) * (HEAD_DIM ** -0.5)
    k = _head_norm_rope(k_ref[...], gk_ref[...], cos2, sin2, gmat)
    v = v_ref[...]
    kc = jnp.concatenate([kprev[...], k], axis=0).astype(BF16)
    vc = jnp.concatenate([vprev[...], v], axis=0).astype(BF16)
    qb = q.astype(BF16)
    bias = bias_ref[jnp.minimum(n, 1)]

    for g in range(N_KV_HEADS):
        qg = jnp.concatenate(
            [qb[:, (g * GQA + h) * HEAD_DIM:(g * GQA + h + 1) * HEAD_DIM] for h in range(GQA)], axis=0)
        kg = kc[:, g * HEAD_DIM:(g + 1) * HEAD_DIM]
        lg_scr[g * rows:(g + 1) * rows, :] = (
            lax.dot_general(qg, kg, (((1,), (1,)), ((), ())), preferred_element_type=F32) + bias)
    lg = lg_scr[...]
    sink = sink_ref[...]
    m = jnp.maximum(jnp.max(lg, axis=-1, keepdims=True), sink)
    e = jnp.exp(lg - jnp.concatenate([m, m], axis=1))
    den = jnp.sum(e, axis=-1, keepdims=True) + jnp.exp(sink - m)
    p_scr[...] = (e / jnp.concatenate([den, den], axis=1)).astype(BF16)
    pieces = []
    for g in range(N_KV_HEADS):
        vg = vc[:, g * HEAD_DIM:(g + 1) * HEAD_DIM]
        og = jnp.dot(p_scr[g * rows:(g + 1) * rows, :], vg, preferred_element_type=F32)
        pieces.extend(og[h * blk:(h + 1) * blk] for h in range(GQA))
    ao = jnp.concatenate(pieces, axis=1)
    ao_ref[...] = _rms_rows(ao, gao_ref[...]).astype(BF16)

    kprev[...] = k
    vprev[...] = v

    @pl.when(n == nb - 1)
    def _():
        klast_ref[0] = k


def _attn_prompt(q, k, v, cos2, sin2, gq, gk, sink_col, gao, gmat, batch, seq):
    nb = seq // WINDOW
    row = lambda b, n: (b * nb + n, 0)
    const2 = lambda b, n: (0, 0)
    const3 = lambda b, n: (0, 0, 0)
    i = np.arange(GQA * WINDOW)[:, None] % WINDOW
    j = np.arange(2 * WINDOW)[None, :]
    band = (j > i) & (j <= i + WINDOW)
    bias = np.stack([np.where(band & (j >= WINDOW), 0.0, NEG_INF), np.where(band, 0.0, NEG_INF)]).astype(np.float32)
    return pl.pallas_call(
        functools.partial(_attn_prompt_body, nb=nb),
        grid=(batch, nb),
        in_specs=[
            pl.BlockSpec((WINDOW, ATTN_WIDTH), row),
            pl.BlockSpec((WINDOW, KV_WIDTH), row),
            pl.BlockSpec((WINDOW, KV_WIDTH), row),
            pl.BlockSpec((WINDOW, LANES), lambda b, n: (n, 0)),
            pl.BlockSpec((WINDOW, LANES), lambda b, n: (n, 0)),
            pl.BlockSpec((1, ATTN_WIDTH), const2),
            pl.BlockSpec((1, KV_WIDTH), const2),
            pl.BlockSpec((N_Q_HEADS * WINDOW, LANES), const2),
            pl.BlockSpec((1, ATTN_WIDTH), const2),
            pl.BlockSpec((MXU_DIM, MXU_DIM), const2),
            pl.BlockSpec((2, GQA * WINDOW, 2 * WINDOW), const3),
        ],
        out_specs=[
            pl.BlockSpec((WINDOW, ATTN_WIDTH), row),
            pl.BlockSpec((1, WINDOW, KV_WIDTH), lambda b, n: (b, 0, 0)),
        ],
        out_shape=[
            jax.ShapeDtypeStruct((batch * seq, ATTN_WIDTH), BF16),
            jax.ShapeDtypeStruct((batch, WINDOW, KV_WIDTH), F32),
        ],
        scratch_shapes=[
            pltpu.VMEM((WINDOW, KV_WIDTH), F32),
            pltpu.VMEM((WINDOW, KV_WIDTH), F32),
            pltpu.VMEM((N_Q_HEADS * WINDOW, 2 * WINDOW), F32),
            pltpu.VMEM((N_Q_HEADS * WINDOW, 2 * WINDOW), BF16),
        ],
        compiler_params=_cparams(("arbitrary", "arbitrary")),
        name="attn_prompt",
    )(q, k, v, cos2, sin2, gq, gk, jnp.broadcast_to(sink_col.reshape(N_Q_HEADS * WINDOW, 1), (N_Q_HEADS * WINDOW, LANES)),
      gao, gmat, jnp.asarray(bias))


def _attn_sample_body(q_ref, k_ref, v_ref, ck_ref, cv_ref, cos_ref, sin_ref, gq_ref, gk_ref, sink_ref, gao_ref,
                      gmat_ref, ao_ref, ks_ref, vs_ref, *, bb, s):
    gmat = gmat_ref[...]
    cos2 = cos_ref[...]
    sin2 = sin_ref[...]
    q = _head_norm_rope(q_ref[...], gq_ref[...], cos2, sin2, gmat) * (HEAD_DIM ** -0.5)
    k = _head_norm_rope(k_ref[...], gk_ref[...], cos2, sin2, gmat)
    v = v_ref[...]
    ck = ck_ref[...]
    cv = cv_ref[...]
    k3 = k.reshape(bb, s, KV_WIDTH)
    v3 = v.reshape(bb, s, KV_WIDTH)
    ks_ref[:, :WINDOW - s, :] = ck[:, s:, :]
    ks_ref[:, WINDOW - s:, :] = k3
    vs_ref[:, :WINDOW - s, :] = cv[:, s:, :]
    vs_ref[:, WINDOW - s:, :] = v3

    q3 = q.astype(BF16).reshape(bb, s, ATTN_WIDTH)
    ckb = ck.astype(BF16)
    cvb = cv.astype(BF16)
    k3b = k3.astype(BF16)
    v3b = v3.astype(BF16)
    rows = GQA * s
    qi = lax.broadcasted_iota(jnp.int32, (bb, rows, WINDOW), 1) & (s - 1)
    tc = lax.broadcasted_iota(jnp.int32, (bb, rows, WINDOW), 2)
    mask_c = tc > qi
    qi2 = lax.broadcasted_iota(jnp.int32, (bb, rows, s), 1) & (s - 1)
    tn = lax.broadcasted_iota(jnp.int32, (bb, rows, s), 2)
    mask_n = tn <= qi2

    pieces = []
    for g in range(N_KV_HEADS):
        qg = jnp.concatenate(
            [q3[:, :, (g * GQA + h) * HEAD_DIM:(g * GQA + h + 1) * HEAD_DIM] for h in range(GQA)], axis=1)
        sl = slice(g * HEAD_DIM, (g + 1) * HEAD_DIM)
        lc = jnp.einsum("bqd,bkd->bqk", qg, ckb[:, :, sl], preferred_element_type=F32)
        ln = jnp.einsum("bqd,bkd->bqk", qg, k3b[:, :, sl], preferred_element_type=F32)
        lc = jnp.where(mask_c, lc, NEG_INF)
        ln = jnp.where(mask_n, ln, NEG_INF)
        sink = sink_ref[g]
        m = jnp.maximum(jnp.maximum(jnp.max(lc, axis=-1, keepdims=True), jnp.max(ln, axis=-1, keepdims=True)), sink)
        ec = jnp.exp(lc - m)
        en = jnp.exp(ln - m)
        den = jnp.sum(ec, axis=-1, keepdims=True) + jnp.sum(en, axis=-1, keepdims=True) + jnp.exp(sink - m)
        og = (jnp.einsum("bqk,bkd->bqd", (ec / den).astype(BF16), cvb[:, :, sl], preferred_element_type=F32)
              + jnp.einsum("bqk,bkd->bqd", (en / den).astype(BF16), v3b[:, :, sl], preferred_element_type=F32))
        pieces.extend(og[:, h * s:(h + 1) * s, :] for h in range(GQA))
    ao = jnp.concatenate(pieces, axis=2).reshape(bb * s, ATTN_WIDTH)
    ao_ref[...] = _rms_rows(ao, gao_ref[...]).astype(BF16)


def _attn_sample(q, k, v, ck, cv, cos2, sin2, gq, gk, sink_col, gao, gmat, batch, s, bb):
    rows = bb * s
    row = lambda i: (i, 0)
    const2 = lambda i: (0, 0)
    b3 = lambda i: (i, 0, 0)
    return pl.pallas_call(
        functools.partial(_attn_sample_body, bb=bb, s=s),
        grid=(batch // bb,),
        in_specs=[
            pl.BlockSpec((rows, ATTN_WIDTH), row),
            pl.BlockSpec((rows, KV_WIDTH), row),
            pl.BlockSpec((rows, KV_WIDTH), row),
            pl.BlockSpec((bb, WINDOW, KV_WIDTH), b3),
            pl.BlockSpec((bb, WINDOW, KV_WIDTH), b3),
            pl.BlockSpec((rows, LANES), const2),
            pl.BlockSpec((rows, LANES), const2),
            pl.BlockSpec((1, ATTN_WIDTH), const2),
            pl.BlockSpec((1, KV_WIDTH), const2),
            pl.BlockSpec((N_KV_HEADS, GQA * s, 1), lambda i: (0, 0, 0)),
            pl.BlockSpec((1, ATTN_WIDTH), const2),
            pl.BlockSpec((MXU_DIM, MXU_DIM), const2),
        ],
        out_specs=[
            pl.BlockSpec((rows, ATTN_WIDTH), row),
            pl.BlockSpec((bb, WINDOW, KV_WIDTH), b3),
            pl.BlockSpec((bb, WINDOW, KV_WIDTH), b3),
        ],
        out_shape=[
            jax.ShapeDtypeStruct((batch * s, ATTN_WIDTH), BF16),
            jax.ShapeDtypeStruct((batch, WINDOW, KV_WIDTH), F32),
            jax.ShapeDtypeStruct((batch, WINDOW, KV_WIDTH), F32),
        ],
        compiler_params=_cparams(("arbitrary",)),
        name="attn_sample",
    )(q, k, v, ck, cv, cos2, sin2, gq, gk, sink_col, gao, gmat)


def _lru_gates(xc, wa_ref, wx_ref, ba, bx, lam):
    xb = xc.astype(BF16)
    r_cols, i_cols = [], []
    for c in range(LRU_WIDTH // MXU_DIM):
        sl = slice(c * MXU_DIM, (c + 1) * MXU_DIM)
        r_cols.append(jnp.dot(xb[:, sl], wa_ref[c], preferred_element_type=F32))
        i_cols.append(jnp.dot(xb[:, sl], wx_ref[c], preferred_element_type=F32))
    r = 1.0 / (1.0 + jnp.exp(-(jnp.concatenate(r_cols, axis=1) + ba)))
    ig = 1.0 / (1.0 + jnp.exp(-(jnp.concatenate(i_cols, axis=1) + bx)))
    neg_lam = -lam
    softplus = jnp.maximum(neg_lam, 0.0) + jnp.log1p(jnp.exp(-jnp.abs(neg_lam)))
    log_a = -LRU_C * r * softplus
    a = jnp.exp(log_a)
    u = jnp.sqrt(-jnp.tanh(log_a) * (a * a + 1.0)) * (ig * xc)
    return a, u


def _segment_scan(a, u, seg):
    t = lax.broadcasted_iota(jnp.int32, a.shape, 0) & (seg - 1)
    d = 1
    while d < seg:
        valid = t >= d
        u = jnp.where(valid, a * pltpu.roll(u, d, axis=0) + u, u)
        a = jnp.where(valid, a * pltpu.roll(a, d, axis=0), a)
        d *= 2
    return a, u


def _lru_prompt_body(xr_ref, yg_ref, cw_ref, cb_ref, wa_ref, wx_ref, ba_ref, bx_ref, lam_ref, glo_ref,
                     lo_ref, hlast_ref, prev8, hcarry, *, tb):
    t_blk = pl.program_id(1)

    @pl.when(t_blk == 0)
    def _():
        prev8[...] = jnp.zeros_like(prev8)
        hcarry[...] = jnp.zeros_like(hcarry)

    xr = xr_ref[...]
    p8 = prev8[...]
    row8 = lax.broadcasted_iota(jnp.int32, p8.shape, 0)
    xc = xr * cw_ref[CONV_W - 1:CONV_W, :] + cb_ref[...]
    for kshift in range(1, CONV_W):
        rolled = pltpu.roll(xr, kshift, axis=0)
        top = jnp.where(row8 < kshift, pltpu.roll(p8, kshift, axis=0), rolled[:SUBLANES])
        shifted = jnp.concatenate([top, rolled[SUBLANES:]], axis=0)
        xc = xc + shifted * cw_ref[CONV_W - 1 - kshift:CONV_W - kshift, :]
    a, u = _lru_gates(xc, wa_ref, wx_ref, ba_ref[...], bx_ref[...], lam_ref[...])
    acum, hloc = _segment_scan(a, u, tb)
    h = hloc + acum * hcarry[...]
    y = h * jax.nn.gelu(yg_ref[...])
    lo_ref[...] = _rms_rows(y, glo_ref[...]).astype(BF16)
    hlast = h[tb - 1:tb, :]
    hcarry[...] = hlast
    prev8[...] = xr[tb - SUBLANES:, :]
    hlast_ref[0] = hlast


def _lru_prompt(xr, yg, cw, cb, wa_bd, wx_bd, ba, bx, lam, glo, batch, seq, tb):
    nt = seq // tb
    row = lambda b, t: (b * nt + t, 0)
    const2 = lambda b, t: (0, 0)
    const3 = lambda b, t: (0, 0, 0)
    nchunk = LRU_WIDTH // MXU_DIM
    return pl.pallas_call(
        functools.partial(_lru_prompt_body, tb=tb),
        grid=(batch, nt),
        in_specs=[
            pl.BlockSpec((tb, LRU_WIDTH), row),
            pl.BlockSpec((tb, LRU_WIDTH), row),
            pl.BlockSpec((CONV_W, LRU_WIDTH), const2),
            pl.BlockSpec((1, LRU_WIDTH), const2),
            pl.BlockSpec((nchunk, MXU_DIM, MXU_DIM), const3),
            pl.BlockSpec((nchunk, MXU_DIM, MXU_DIM), const3),
            pl.BlockSpec((1, LRU_WIDTH), const2),
            pl.BlockSpec((1, LRU_WIDTH), const2),
            pl.BlockSpec((1, LRU_WIDTH), const2),
            pl.BlockSpec((1, LRU_WIDTH), const2),
        ],
        out_specs=[
            pl.BlockSpec((tb, LRU_WIDTH), row),
            pl.BlockSpec((1, 1, LRU_WIDTH), lambda b, t: (b, 0, 0)),
        ],
        out_shape=[
            jax.ShapeDtypeStruct((batch * seq, LRU_WIDTH), BF16),
            jax.ShapeDtypeStruct((batch, 1, LRU_WIDTH), F32),
        ],
        scratch_shapes=[pltpu.VMEM((SUBLANES, LRU_WIDTH), F32), pltpu.VMEM((1, LRU_WIDTH), F32)],
        compiler_params=_cparams(("arbitrary", "arbitrary")),
        name="lru_prompt",
    )(xr, yg, cw, cb, wa_bd, wx_bd, ba, bx, lam, glo)


def _lru_sample_body(xr_ref, yg_ref, cpad_ref, h0_ref, cw_ref, cb_ref, wa_ref, wx_ref, ba_ref, bx_ref, lam_ref,
                     glo_ref, lo_ref, h_ref, *, s):
    xr = xr_ref[...]
    cpad = cpad_ref[...]
    rows = xr.shape[0]
    t = lax.broadcasted_iota(jnp.int32, xr.shape, 0) & (s - 1)
    xc = xr * cw_ref[CONV_W - 1:CONV_W, :] + cb_ref[...]
    for kshift in range(1, CONV_W):
        shifted = jnp.where(t < kshift, pltpu.roll(cpad, rows - s + kshift, axis=0), pltpu.roll(xr, kshift, axis=0))
        xc = xc + shifted * cw_ref[CONV_W - 1 - kshift:CONV_W - kshift, :]
    a, u = _lru_gates(xc, wa_ref, wx_ref, ba_ref[...], bx_ref[...], lam_ref[...])
    acum, hloc = _segment_scan(a, u, s)
    h = hloc + acum * h0_ref[...]
    h_ref[...] = h
    y = h * jax.nn.gelu(yg_ref[...])
    lo_ref[...] = _rms_rows(y, glo_ref[...]).astype(BF16)


def _lru_sample(xr, yg, cpad, h0rows, cw, cb, wa_bd, wx_bd, ba, bx, lam, glo, s, tb):
    n = xr.shape[0]
    row = lambda i: (i, 0)
    const2 = lambda i: (0, 0)
    const3 = lambda i: (0, 0, 0)
    nchunk = LRU_WIDTH // MXU_DIM
    return pl.pallas_call(
        functools.partial(_lru_sample_body, s=s),
        grid=(n // tb,),
        in_specs=[
            pl.BlockSpec((tb, LRU_WIDTH), row),
            pl.BlockSpec((tb, LRU_WIDTH), row),
            pl.BlockSpec((tb, LRU_WIDTH), row),
            pl.BlockSpec((tb, LRU_WIDTH), row),
            pl.BlockSpec((CONV_W, LRU_WIDTH), const2),
            pl.BlockSpec((1, LRU_WIDTH), const2),
            pl.BlockSpec((nchunk, MXU_DIM, MXU_DIM), const3),
            pl.BlockSpec((nchunk, MXU_DIM, MXU_DIM), const3),
            pl.BlockSpec((1, LRU_WIDTH), const2),
            pl.BlockSpec((1, LRU_WIDTH), const2),
            pl.BlockSpec((1, LRU_WIDTH), const2),
            pl.BlockSpec((1, LRU_WIDTH), const2),
        ],
        out_specs=[pl.BlockSpec((tb, LRU_WIDTH), row), pl.BlockSpec((tb, LRU_WIDTH), row)],
        out_shape=[jax.ShapeDtypeStruct((n, LRU_WIDTH), BF16), jax.ShapeDtypeStruct((n, LRU_WIDTH), F32)],
        compiler_params=_cparams(("arbitrary",)),
        name="lru_sample",
    )(xr, yg, cpad, h0rows, cw, cb, wa_bd, wx_bd, ba, bx, lam, glo)


def _outproj_body(x_ref, ao_ref, lo_ref, w_ref, g_ref, x1_ref, hnt_ref):
    mixed = (jnp.dot(ao_ref[...], w_ref[:ATTN_WIDTH, :], preferred_element_type=F32)
             + jnp.dot(lo_ref[...], w_ref[ATTN_WIDTH:, :], preferred_element_type=F32))
    x1 = x_ref[...] + mixed
    x1_ref[...] = x1
    hnt_ref[...] = _rms_rows(x1, g_ref[...]).T.astype(BF16)


def _outproj(x, ao, lo, w_bf, g, tm):
    n = x.shape[0]
    row = lambda i: (i, 0)
    const2 = lambda i: (0, 0)
    return pl.pallas_call(
        _outproj_body,
        grid=(n // tm,),
        in_specs=[
            pl.BlockSpec((tm, D_MODEL), row),
            pl.BlockSpec((tm, ATTN_WIDTH), row),
            pl.BlockSpec((tm, LRU_WIDTH), row),
            pl.BlockSpec((D_MODEL, D_MODEL), const2),
            pl.BlockSpec((1, D_MODEL), const2),
        ],
        out_specs=[pl.BlockSpec((tm, D_MODEL), row), pl.BlockSpec((D_MODEL, tm), lambda i: (0, i))],
        out_shape=[jax.ShapeDtypeStruct((n, D_MODEL), F32), jax.ShapeDtypeStruct((D_MODEL, n), BF16)],
        compiler_params=_cparams(("arbitrary",)),
        name="outproj",
    )(x, ao, lo, w_bf, g)


PEER_CAND_COLS = tuple(min(PEER_TOPK, (PEER_TOPK + 1) // (a + 1)) for a in range(PEER_TOPK))
ROUTE_LANES = 128
ROUTE_TILES_PER_STEP = 2


def _sort_network(n):
    pairs, p = [], 1
    while p < n:
        k = p
        while k >= 1:
            for j in range(k % p, n - k, 2 * k):
                for i in range(min(k, n - j - k)):
                    if (i + j) // (2 * p) == (i + j + k) // (2 * p):
                        pairs.append((i + j, i + j + k))
            k //= 2
        p *= 2
    return tuple(pairs)


SORT16 = _sort_network(N_KEYS // SUBLANES)


def _pop_lists(heads_and_ids, nvals):
    levels = [list(lv) for lv, _ in heads_and_ids]
    ids = [i for _, i in heads_and_ids]
    big = float(SUBLANES * len(levels))
    vals = []
    for r in range(nvals):
        top = levels[0][0]
        for lv in levels[1:]:
            top = jnp.maximum(top, lv[0])
        m = jnp.max(top, axis=0, keepdims=True)
        vals.append(m)
        if r == nvals - 1:
            break
        first = None
        for lv, i in zip(levels, ids):
            cand = jnp.where(lv[0] == m, i, big)
            first = cand if first is None else jnp.minimum(first, cand)
        first = jnp.min(first, axis=0, keepdims=True)
        for lv, i in zip(levels, ids):
            hit = i == first
            live = min(len(lv), nvals - 1 - r)
            for d in range(live):
                below = lv[d + 1] if d + 1 < len(lv) else -jnp.inf
                lv[d] = jnp.where(hit, below, lv[d])
    return vals


def _top16(s, ltri, ones8):
    lanes = s.shape[1]
    lv = [s[SUBLANES * r:SUBLANES * (r + 1)] for r in range(N_KEYS // SUBLANES)]
    for i, j in SORT16:
        lv[i], lv[j] = jnp.maximum(lv[i], lv[j]), jnp.minimum(lv[i], lv[j])
    subf = lax.broadcasted_iota(jnp.int32, (SUBLANES, lanes), 0).astype(F32)
    vals = _pop_lists([(lv, subf)], PEER_TOPK)
    v16 = vals[PEER_TOPK - 1]
    gt = s > v16
    eq = s == v16
    n_gt = jnp.dot(ones8, jnp.where(gt, 1.0, 0.0).astype(BF16), preferred_element_type=F32)[0:1]
    eq_before = jnp.dot(ltri, jnp.where(eq, 1.0, 0.0).astype(BF16), preferred_element_type=F32)
    need = float(PEER_TOPK) - n_gt
    kept = jnp.where(gt, s, jnp.where(eq, jnp.where(eq_before < need, s, -jnp.inf), -jnp.inf))
    return vals, kept


def _best_sums(v0, v1):
    lanes = v0[0].shape[1]
    lo0 = jnp.concatenate(v0[:SUBLANES], axis=0)
    hi0 = jnp.concatenate(v0[SUBLANES:], axis=0)
    row8 = lax.broadcasted_iota(jnp.int32, (SUBLANES, lanes), 0)
    subf = row8.astype(F32)
    levels = []
    for b in range(PEER_TOPK):
        c = lo0 + v1[b]
        nrows = sum(1 for a in range(SUBLANES) if PEER_CAND_COLS[a] > b)
        levels.append(c if nrows == SUBLANES else jnp.where(row8 < nrows, c, -jnp.inf))
    return _pop_lists([(levels, subf), ([hi0 + v1[0]], subf + float(SUBLANES))], PEER_TOPK + 1)


def _route_body(hnt_ref, wq_ref, keys_ref, ltri_ref, ones_ref, s1m_ref, e1z_ref, t0_ref, e0_ref, s_scr, *, tm):
    qt = jnp.dot(wq_ref[...], hnt_ref[...], preferred_element_type=F32)
    half = D_KEY // 2
    for hc in range(2 * PEER_HEADS):
        qhc = qt[hc * half:(hc + 1) * half, :].astype(BF16)
        s_scr[hc] = jnp.dot(keys_ref[hc], qhc, preferred_element_type=F32)

    def route_tile(h, lanes):
        ltri = ltri_ref[...]
        ones8 = ones_ref[...]
        v0, s0m = _top16(s_scr[2 * h, :, lanes], ltri, ones8)
        v1, s1m = _top16(s_scr[2 * h + 1, :, lanes], ltri, ones8)
        best = _best_sums(v0, v1)
        mx = best[0]
        z = jnp.exp(best[0] - mx)
        for r in range(1, PEER_TOPK):
            z = z + jnp.exp(best[r] - mx)
        thr = 0.5 * (best[PEER_TOPK - 1] + best[PEER_TOPK])
        s1m_ref[h, :, lanes] = s1m
        e1z_ref[h, :, lanes] = jnp.exp(s1m - v1[0]) / z
        t0_ref[h, :, lanes] = thr - s0m
        e0_ref[h, :, lanes] = jnp.exp(s0m - v0[0])

    per_head = tm // (ROUTE_TILES_PER_STEP * ROUTE_LANES)

    def step(it, carry):
        h = it // per_head
        c = it % per_head
        for t in range(ROUTE_TILES_PER_STEP):
            start = (c * ROUTE_TILES_PER_STEP + t) * ROUTE_LANES
            route_tile(h, pl.ds(pl.multiple_of(start, ROUTE_LANES), ROUTE_LANES))
        return carry

    lax.fori_loop(0, PEER_HEADS * per_head, step, 0)


def _route(hnt, wq_t, keys, tm):
    n = hnt.shape[1]
    blk = lambda i: (0, 0, i)
    shape = jax.ShapeDtypeStruct((PEER_HEADS, N_KEYS, n), F32)
    spec = pl.BlockSpec((PEER_HEADS, N_KEYS, tm), blk)
    ltri = jnp.asarray(np.tril(np.ones((N_KEYS, N_KEYS)), -1), BF16)
    ones8 = jnp.ones((SUBLANES, N_KEYS), BF16)
    return pl.pallas_call(
        functools.partial(_route_body, tm=tm),
        grid=(n // tm,),
        in_specs=[
            pl.BlockSpec((D_MODEL, tm), lambda i: (0, i)),
            pl.BlockSpec((PEER_HEADS * D_KEY, D_MODEL), lambda i: (0, 0)),
            pl.BlockSpec((2 * PEER_HEADS, N_KEYS, D_KEY // 2), lambda i: (0, 0, 0)),
            pl.BlockSpec((N_KEYS, N_KEYS), lambda i: (0, 0)),
            pl.BlockSpec((SUBLANES, N_KEYS), lambda i: (0, 0)),
        ],
        out_specs=[spec, spec, spec, spec],
        out_shape=[shape, shape, shape, shape],
        scratch_shapes=[pltpu.VMEM((2 * PEER_HEADS, N_KEYS, tm), F32)],
        compiler_params=_cparams(("arbitrary",)),
        name="peer_route",
    )(hnt, wq_t, keys, ltri, ones8)


EXPERT_LANES = 256
EXPERT_BLOCK = SUBLANES * N_KEYS
EXPERT_ROWS = EXPERT_BLOCK // N_KEYS
U_KCHUNK = 512
U_NCHUNK = D_MODEL // U_KCHUNK
V_KCHUNK = 512
V_NCHUNK = EXPERT_BLOCK // V_KCHUNK


def _routing_weight(s1m_ref, e1z_ref, t0_ref, e0_ref, k, ii, lanes):
    width = lanes.stop - lanes.start
    trow = [jnp.broadcast_to(t0_ref[h, k, ii:ii + 1, lanes], (SUBLANES, width)) for h in range(PEER_HEADS)]
    e0row = [jnp.broadcast_to(e0_ref[h, k, ii:ii + 1, lanes], (SUBLANES, width)) for h in range(PEER_HEADS)]
    out = []
    for j in range(0, N_KEYS, SUBLANES):
        w = None
        for h in range(PEER_HEADS):
            term = jnp.where(s1m_ref[h, j:j + SUBLANES, lanes] >= trow[h], e1z_ref[h, j:j + SUBLANES, lanes], 0.0) * e0row[h]
            w = term if w is None else w + term
        out.append(w)
    return jnp.concatenate(out, axis=0)


def _experts_body(hnt_ref, *refs, tm, nk):
    u_refs = refs[:U_NCHUNK]
    vt_refs = refs[U_NCHUNK:U_NCHUNK + V_NCHUNK]
    (s1m_ref, e1z_ref, t0_ref, e0_ref, x1_ref, y_ref, acc, act, wbuf, pt_a, pt_b) = refs[U_NCHUNK + V_NCHUNK:]
    k = pl.program_id(1)
    kt = jnp.minimum(k, nk - 1)
    nlane = tm // EXPERT_LANES

    def weights():
        for ii in range(EXPERT_ROWS):
            for c in range(nlane):
                rows = slice(ii * N_KEYS, (ii + 1) * N_KEYS)
                lanes = slice(c * EXPERT_LANES, (c + 1) * EXPERT_LANES)
                wbuf[rows, lanes] = _routing_weight(s1m_ref, e1z_ref, t0_ref, e0_ref, kt, ii, lanes)

    def dot1():
        for c in range(nlane):
            lanes = slice(c * EXPERT_LANES, (c + 1) * EXPERT_LANES)
            a = None
            for kc in range(U_NCHUNK):
                d = jnp.dot(u_refs[kc][...], hnt_ref[kc * U_KCHUNK:(kc + 1) * U_KCHUNK, lanes],
                            preferred_element_type=F32)
                a = d if a is None else a + d
            act[:, lanes] = a

    def weighted_act(pt):
        for ii in range(EXPERT_ROWS):
            rows = slice(ii * N_KEYS, (ii + 1) * N_KEYS)
            for c in range(nlane):
                lanes = slice(c * EXPERT_LANES, (c + 1) * EXPERT_LANES)
                pt[rows, lanes] = (wbuf[rows, lanes] * jax.nn.gelu(act[rows, lanes])).astype(BF16)

    def dot2(pt):
        a = None
        for c in range(V_NCHUNK):
            d = jnp.dot(vt_refs[c][...], pt[c * V_KCHUNK:(c + 1) * V_KCHUNK, :], preferred_element_type=F32)
            a = d if a is None else a + d
        acc[...] += a

    @pl.when(k == 0)
    def _():
        acc[...] = jnp.zeros_like(acc)
        pt_b[...] = jnp.zeros_like(pt_b)

    @pl.when(k < nk)
    def _():
        weights()
        dot1()

    def phase2(pt_new, pt_old):
        weighted_act(pt_new)
        dot2(pt_old)

    @pl.when(k % 2 == 0)
    def _():
        phase2(pt_a, pt_b)

    @pl.when(k % 2 == 1)
    def _():
        phase2(pt_b, pt_a)

    @pl.when(k == nk)
    def _():
        y_ref[...] = x1_ref[...] + acc[...].T


def _experts(hnt, u_bf, vt_bf, s1m, e1z, t0, e0, x1, tm):
    n = hnt.shape[1]
    nk = N_EXPERTS // EXPERT_BLOCK
    assert nk % 2 == 0
    tok = lambda t, k: (t, 0)
    once = pl.Buffered(1)
    small = pl.BlockSpec((PEER_HEADS, N_KEYS, tm), lambda t, k: (0, 0, t), pipeline_mode=once)
    tiled = pl.BlockSpec((PEER_HEADS, N_KEYS // SUBLANES, SUBLANES, tm), lambda t, k: (0, 0, 0, t), pipeline_mode=once)
    t0 = t0.reshape(PEER_HEADS, N_KEYS // SUBLANES, SUBLANES, n)
    e0 = e0.reshape(PEER_HEADS, N_KEYS // SUBLANES, SUBLANES, n)
    u_specs = [pl.BlockSpec((EXPERT_BLOCK, U_KCHUNK), lambda t, k, kc=kc: (jnp.minimum(k, nk - 1), kc))
               for kc in range(U_NCHUNK)]
    vt_specs = [pl.BlockSpec((D_MODEL, V_KCHUNK), lambda t, k, c=c: (0, V_NCHUNK * jnp.maximum(k - 1, 0) + c))
                for c in range(V_NCHUNK)]
    return pl.pallas_call(
        functools.partial(_experts_body, tm=tm, nk=nk),
        grid=(n // tm, nk + 1),
        in_specs=[
            pl.BlockSpec((D_MODEL, tm), lambda t, k: (0, t), pipeline_mode=once),
            *u_specs,
            *vt_specs,
            small, small, tiled, tiled,
            pl.BlockSpec((tm, D_MODEL), tok, pipeline_mode=once),
        ],
        out_specs=pl.BlockSpec((tm, D_MODEL), tok),
        out_shape=jax.ShapeDtypeStruct((n, D_MODEL), F32),
        scratch_shapes=[
            pltpu.VMEM((D_MODEL, tm), F32),
            pltpu.VMEM((EXPERT_BLOCK, tm), F32),
            pltpu.VMEM((EXPERT_BLOCK, tm), F32),
            pltpu.VMEM((EXPERT_BLOCK, tm), BF16),
            pltpu.VMEM((EXPERT_BLOCK, tm), BF16),
        ],
        compiler_params=_cparams(("arbitrary", "arbitrary")),
        name="peer_experts",
    )(hnt, *([u_bf] * len(u_specs)), *([vt_bf] * len(vt_specs)), s1m, e1z, t0, e0, x1)


def _rope_tables(pos):
    half = HEAD_DIM // 2
    inv = jnp.exp(-math.log(ROPE_THETA) * jnp.arange(half, dtype=F32) * (2.0 / HEAD_DIM))
    ang = pos[:, None] * inv[None, :]
    cos, sin = jnp.cos(ang), jnp.sin(ang)
    cos2 = jnp.concatenate([cos, cos, cos, cos], axis=1)
    sin2 = jnp.concatenate([-sin, sin, -sin, sin], axis=1)
    return cos2, sin2


def _block_diag(w):
    per = MXU_DIM // LRU_BLOCK
    w4 = w.reshape(LRU_WIDTH // MXU_DIM, per, LRU_BLOCK, LRU_BLOCK)
    eye = jnp.eye(per, dtype=w.dtype)
    return jnp.einsum("cpde,pq->cpdqe", w4, eye).reshape(LRU_WIDTH // MXU_DIM, MXU_DIM, MXU_DIM).astype(BF16)


def kernel(x_prompt, x_sample, cache_k, cache_v, state_conv, state_h, norm_mix_g, w_in, q_norm_g, k_norm_g, attn_sinks, conv_w, conv_b, w_rec_gate, b_rec_gate, w_in_gate, b_in_gate, lru_lambda, attn_out_g, lru_out_g, w_out, norm_ffn_g, w_peer_q, peer_sub_keys, peer_u, peer_v):
    depth = w_in.shape[0]
    assert depth == 1
    batch, seq, _ = x_prompt.shape
    dbatch, dseq, _ = x_sample.shape
    past_len = 8192
    l = 0

    w_in_bf = w_in[l].astype(BF16)
    w_out_bf = w_out[l].astype(BF16)
    wq_t = w_peer_q[l].T.astype(BF16)
    keys = peer_sub_keys[l].reshape(2 * PEER_HEADS, N_KEYS, D_KEY // 2).astype(BF16)
    u_bf = peer_u[l].astype(BF16)
    vt_bf = peer_v[l].T.astype(BF16)
    wa_bd = _block_diag(w_rec_gate[l])
    wx_bd = _block_diag(w_in_gate[l])
    ba = b_rec_gate[l].reshape(1, LRU_WIDTH)
    bx = b_in_gate[l].reshape(1, LRU_WIDTH)
    lam = lru_lambda[l].reshape(1, LRU_WIDTH)
    cw = conv_w[l]
    cb = conv_b[l].reshape(1, LRU_WIDTH)
    g_mix = norm_mix_g[l].reshape(1, D_MODEL)
    g_ffn = norm_ffn_g[l].reshape(1, D_MODEL)
    gq = jnp.tile(q_norm_g[l], N_Q_HEADS).reshape(1, ATTN_WIDTH)
    gk = jnp.tile(k_norm_g[l], N_KV_HEADS).reshape(1, KV_WIDTH)
    gao = attn_out_g[l].reshape(1, ATTN_WIDTH)
    glo = lru_out_g[l].reshape(1, LRU_WIDTH)
    gmat = jnp.asarray(np.kron(np.eye(MXU_DIM // HEAD_DIM), np.full((HEAD_DIM, HEAD_DIM), 1.0 / HEAD_DIM)), BF16)
    sinks = attn_sinks[l].astype(F32).reshape(N_KV_HEADS, GQA)
    sink_p = jnp.repeat(sinks, WINDOW, axis=1).reshape(N_KV_HEADS, GQA * WINDOW, 1)
    sink_s = jnp.repeat(sinks, dseq, axis=1).reshape(N_KV_HEADS, GQA * dseq, 1)
    cos_p, sin_p = _rope_tables(jnp.arange(seq, dtype=F32))
    cos_s, sin_s = _rope_tables(float(past_len) + jnp.arange(dseq, dtype=F32))

    n_p = batch * seq
    n_s = dbatch * dseq
    xp2 = x_prompt.reshape(n_p, D_MODEL)
    xs2 = x_sample.reshape(n_s, D_MODEL)

    qp, kp, vp, xrp, ygp = _inproj(xp2, g_mix, w_in_bf, 256)
    ao_p, k_last = _attn_prompt(qp, kp, vp, cos_p, sin_p, gq, gk, sink_p, gao, gmat, batch, seq)
    lo_p, h_p = _lru_prompt(xrp, ygp, cw, cb, wa_bd, wx_bd, ba, bx, lam, glo, batch, seq, 256)
    x1p, hn_p = _outproj(xp2, ao_p, lo_p, w_out_bf, g_ffn, 256)

    bb = 8
    qs, ks, vs, xrs, ygs = _inproj(xs2, g_mix, w_in_bf, 256)
    cos_sb = jnp.tile(cos_s, (bb, 1))
    sin_sb = jnp.tile(sin_s, (bb, 1))
    ck = cache_k[l].reshape(dbatch, WINDOW, KV_WIDTH)
    cv = cache_v[l].reshape(dbatch, WINDOW, KV_WIDTH)
    ao_s, k_s, v_s = _attn_sample(qs, ks, vs, ck, cv, cos_sb, sin_sb, gq, gk, sink_s, gao, gmat, dbatch, dseq, bb)
    cpad = jnp.pad(state_conv[l], ((0, 0), (dseq - (CONV_W - 1), 0), (0, 0))).reshape(n_s, LRU_WIDTH)
    h0rows = jnp.repeat(state_h[l], dseq, axis=0)
    lo_s, h_s = _lru_sample(xrs, ygs, cpad, h0rows, cw, cb, wa_bd, wx_bd, ba, bx, lam, glo, dseq, 256)
    x1s, hn_s = _outproj(xs2, ao_s, lo_s, w_out_bf, g_ffn, 256)

    tm = 512
    outs = []
    for x1, hn in ((x1p, hn_p), (x1s, hn_s)):
        s1m, e1z, t0, e0 = _route(hn, wq_t, keys, tm)
        outs.append(_experts(hn, u_bf, vt_bf, s1m, e1z, t0, e0, x1, tm))
    y_p = outs[0].reshape(batch, seq, D_MODEL)
    y_s = outs[1].reshape(dbatch, dseq, D_MODEL)

    kv_shape = (N_KV_HEADS, HEAD_DIM)
    k_prompt = k_last.reshape(1, batch, WINDOW, *kv_shape)
    v_prompt = vp.reshape(batch, seq, *kv_shape)[:, -WINDOW:][None]
    conv_prompt = xrp.reshape(batch, seq, LRU_WIDTH)[:, -(CONV_W - 1):][None]
    h_prompt = h_p.reshape(1, batch, LRU_WIDTH)
    k_sample = k_s.reshape(1, dbatch, WINDOW, *kv_shape)
    v_sample = v_s.reshape(1, dbatch, WINDOW, *kv_shape)
    conv_sample = xrs.reshape(dbatch, dseq, LRU_WIDTH)[:, -(CONV_W - 1):][None]
    h_sample = h_s.reshape(dbatch, dseq, LRU_WIDTH)[:, -1][None]
    return (y_p, y_s, k_prompt, v_prompt, conv_prompt, h_prompt, k_sample, v_sample, conv_sample, h_sample)
```

```python
import functools
import math

import jax
import jax.numpy as jnp
import numpy as np
from jax import lax
from jax.experimental import pallas as pl
from jax.experimental.pallas import tpu as pltpu

F32 = jnp.float32
BF16 = jnp.bfloat16

D_MODEL = 2048
HEAD_DIM = 64
N_Q_HEADS = 16
N_KV_HEADS = 4
GQA = N_Q_HEADS // N_KV_HEADS
ATTN_WIDTH = N_Q_HEADS * HEAD_DIM
KV_WIDTH = N_KV_HEADS * HEAD_DIM
WINDOW = 128
ROPE_THETA = 10000.0
LRU_WIDTH = D_MODEL - ATTN_WIDTH
LRU_BLOCK = 64
CONV_W = 4
LRU_C = 8.0
N_KEYS = 128
N_EXPERTS = N_KEYS * N_KEYS
PEER_HEADS = 8
PEER_TOPK = 16
D_KEY = 256
EPS = 1e-6
NEG_INF = -1e30

MXU_DIM = 256
LANES = 128
SUBLANES = 8
VMEM_LIMIT_BYTES = 60 * 1024 * 1024


def _cparams(sem, flags=None):
    return pltpu.CompilerParams(dimension_semantics=sem, vmem_limit_bytes=VMEM_LIMIT_BYTES, flags=flags)


IN_CUTS = (0, ATTN_WIDTH, ATTN_WIDTH + KV_WIDTH, ATTN_WIDTH + 2 * KV_WIDTH,
           ATTN_WIDTH + 2 * KV_WIDTH + LRU_WIDTH, ATTN_WIDTH + 2 * KV_WIDTH + 2 * LRU_WIDTH)


def _inproj_body(x_ref, g_ref, w_ref, *refs, with_cast):
    if with_cast:
        src_ref, q_ref, k_ref, v_ref, xr_ref, yg_ref, dst_ref = refs
        dst_ref[...] = src_ref[...].astype(BF16)
    else:
        q_ref, k_ref, v_ref, xr_ref, yg_ref = refs
    x = x_ref[...]
    ms = jnp.mean(x * x, axis=-1, keepdims=True)
    hn = (x * lax.rsqrt(ms + EPS) * g_ref[...]).astype(BF16)
    for o_ref, lo, hi in zip((q_ref, k_ref, v_ref, xr_ref, yg_ref), IN_CUTS[:-1], IN_CUTS[1:]):
        o_ref[...] = jnp.dot(hn, w_ref[:, lo:hi], preferred_element_type=F32)


def _inproj(x, g, w_bf, tm, cast_src=None):
    n = x.shape[0]
    steps = n // tm
    widths = [hi - lo for lo, hi in zip(IN_CUTS[:-1], IN_CUTS[1:])]
    in_specs = [
        pl.BlockSpec((tm, D_MODEL), lambda i: (i, 0)),
        pl.BlockSpec((1, D_MODEL), lambda i: (0, 0)),
        pl.BlockSpec((D_MODEL, IN_CUTS[-1]), lambda i: (0, 0), pipeline_mode=pl.Buffered(1)),
    ]
    out_specs = [pl.BlockSpec((tm, w), lambda i: (i, 0)) for w in widths]
    out_shape = [jax.ShapeDtypeStruct((n, w), F32) for w in widths]
    args = [x, g, w_bf]
    if cast_src is not None:
        rows, cols = cast_src.shape
        slab = pl.BlockSpec((rows // steps, cols), lambda i: (i, 0))
        in_specs.append(slab)
        out_specs.append(slab)
        out_shape.append(jax.ShapeDtypeStruct((rows, cols), BF16))
        args.append(cast_src)
    return pl.pallas_call(
        functools.partial(_inproj_body, with_cast=cast_src is not None),
        grid=(steps,),
        in_specs=in_specs,
        out_specs=out_specs,
        out_shape=out_shape,
        compiler_params=_cparams(("arbitrary",)),
        name="inproj",
    )(*args)


def _group_mean_sq(x, gmat):
    x2 = x * x
    hi = x2.astype(BF16)
    lo = (x2 - hi.astype(F32)).astype(BF16)
    cols = []
    for c in range(x.shape[1] // MXU_DIM):
        sl = slice(c * MXU_DIM, (c + 1) * MXU_DIM)
        cols.append(jnp.dot(hi[:, sl], gmat, preferred_element_type=F32)
                    + jnp.dot(lo[:, sl], gmat, preferred_element_type=F32))
    return cols[0] if len(cols) == 1 else jnp.concatenate(cols, axis=1)


def _tile_lanes(t, reps):
    return t if reps == 1 else jnp.concatenate([t] * reps, axis=1)


def _head_norm_rope(x, gain, cos2, sin2, gmat):
    width = x.shape[1]
    y = x * lax.rsqrt(_group_mean_sq(x, gmat) + EPS) * gain
    lane = lax.broadcasted_iota(jnp.int32, y.shape, 1)
    first_half = (lane & (HEAD_DIM // 2)) == 0
    partner = jnp.where(first_half,
                        pltpu.roll(y, width - HEAD_DIM // 2, axis=1),
                        pltpu.roll(y, HEAD_DIM // 2, axis=1))
    reps = width // LANES
    return y * _tile_lanes(cos2, reps) + partner * _tile_lanes(sin2, reps)


def _rms_rows(x, gain):
    ms = jnp.mean(x * x, axis=-1, keepdims=True)
    return x * lax.rsqrt(ms + EPS) * gain


def _attn_prompt_body(q_ref, k_ref, v_ref, cos_ref, sin_ref, gq_ref, gk_ref, sink_ref, gao_ref, gmat_ref, bias_ref,
                      ao_ref, klast_ref, kprev, vprev, lg_scr, p_scr, *, nb):
    n = pl.program_id(1)
    blk = WINDOW
    rows = GQA * blk

    @pl.when(n == 0)
    def _():
        kprev[...] = jnp.zeros_like(kprev)
        vprev[...] = jnp.zeros_like(vprev)

    gmat = gmat_ref[...]
    cos2 = cos_ref[...]
    sin2 = sin_ref[...]
    q = _head_norm_rope(q_ref[...], gq_ref[...], cos2, sin2, gmat) * (HEAD_DIM ** -0.5)
    k = _head_norm_rope(k_ref[...], gk_ref[...], cos2, sin2, gmat)
    v = v_ref[...]
    kc = jnp.concatenate([kprev[...], k], axis=0).astype(BF16)
    vc = jnp.concatenate([vprev[...], v], axis=0).astype(BF16)
    qb = q.astype(BF16)
    bias = bias_ref[jnp.minimum(n, 1)]

    for g in range(N_KV_HEADS):
        qg = jnp.concatenate(
            [qb[:, (g * GQA + h) * HEAD_DIM:(g * GQA + h + 1) * HEAD_DIM] for h in range(GQA)], axis=0)
        kg = kc[:, g * HEAD_DIM:(g + 1) * HEAD_DIM]
        lg_scr[g * rows:(g + 1) * rows, :] = (
            lax.dot_general(qg, kg, (((1,), (1,)), ((), ())), preferred_element_type=F32) + bias)
    lg = lg_scr[...]
    sink = sink_ref[...]
    m = jnp.maximum(jnp.max(lg, axis=-1, keepdims=True), sink)
    e = jnp.exp(lg - jnp.concatenate([m, m], axis=1))
    den = jnp.sum(e, axis=-1, keepdims=True) + jnp.exp(sink - m)
    p_scr[...] = (e / jnp.concatenate([den, den], axis=1)).astype(BF16)
    pieces = []
    for g in range(N_KV_HEADS):
        vg = vc[:, g * HEAD_DIM:(g + 1) * HEAD_DIM]
        og = jnp.dot(p_scr[g * rows:(g + 1) * rows, :], vg, preferred_element_type=F32)
        pieces.extend(og[h * blk:(h + 1) * blk] for h in range(GQA))
    ao = jnp.concatenate(pieces, axis=1)
    ao_ref[...] = _rms_rows(ao, gao_ref[...]).astype(BF16)

    kprev[...] = k
    vprev[...] = v

    @pl.when(n == nb - 1)
    def _():
        klast_ref[0] = k


def _attn_prompt(q, k, v, cos2, sin2, gq, gk, sink_col, gao, gmat, batch, seq):
    nb = seq // WINDOW
    row = lambda b, n: (b * nb + n, 0)
    const2 = lambda b, n: (0, 0)
    const3 = lambda b, n: (0, 0, 0)
    i = np.arange(GQA * WINDOW)[:, None] % WINDOW
    j = np.arange(2 * WINDOW)[None, :]
    band = (j > i) & (j <= i + WINDOW)
    bias = np.stack([np.where(band & (j >= WINDOW), 0.0, NEG_INF), np.where(band, 0.0, NEG_INF)]).astype(np.float32)
    return pl.pallas_call(
        functools.partial(_attn_prompt_body, nb=nb),
        grid=(batch, nb),
        in_specs=[
            pl.BlockSpec((WINDOW, ATTN_WIDTH), row),
            pl.BlockSpec((WINDOW, KV_WIDTH), row),
            pl.BlockSpec((WINDOW, KV_WIDTH), row),
            pl.BlockSpec((WINDOW, LANES), lambda b, n: (n, 0)),
            pl.BlockSpec((WINDOW, LANES), lambda b, n: (n, 0)),
            pl.BlockSpec((1, ATTN_WIDTH), const2),
            pl.BlockSpec((1, KV_WIDTH), const2),
            pl.BlockSpec((N_Q_HEADS * WINDOW, LANES), const2),
            pl.BlockSpec((1, ATTN_WIDTH), const2),
            pl.BlockSpec((MXU_DIM, MXU_DIM), const2),
            pl.BlockSpec((2, GQA * WINDOW, 2 * WINDOW), const3),
        ],
        out_specs=[
            pl.BlockSpec((WINDOW, ATTN_WIDTH), row),
            pl.BlockSpec((1, WINDOW, KV_WIDTH), lambda b, n: (b, 0, 0)),
        ],
        out_shape=[
            jax.ShapeDtypeStruct((batch * seq, ATTN_WIDTH), BF16),
            jax.ShapeDtypeStruct((batch, WINDOW, KV_WIDTH), F32),
        ],
        scratch_shapes=[
            pltpu.VMEM((WINDOW, KV_WIDTH), F32),
            pltpu.VMEM((WINDOW, KV_WIDTH), F32),
            pltpu.VMEM((N_Q_HEADS * WINDOW, 2 * WINDOW), F32),
            pltpu.VMEM((N_Q_HEADS * WINDOW, 2 * WINDOW), BF16),
        ],
        compiler_params=_cparams(("arbitrary", "arbitrary")),
        name="attn_prompt",
    )(q, k, v, cos2, sin2, gq, gk, jnp.broadcast_to(sink_col.reshape(N_Q_HEADS * WINDOW, 1), (N_Q_HEADS * WINDOW, LANES)),
      gao, gmat, jnp.asarray(bias))


def _attn_sample_body(q_ref, k_ref, v_ref, ck_ref, cv_ref, cos_ref, sin_ref, gq_ref, gk_ref, sink_ref, gao_ref,
                      gmat_ref, ao_ref, ks_ref, vs_ref, *, bb, s):
    gmat = gmat_ref[...]
    cos2 = cos_ref[...]
    sin2 = sin_ref[...]
    q = _head_norm_rope(q_ref[...], gq_ref[...], cos2, sin2, gmat) * (HEAD_DIM ** -0.5)
    k = _head_norm_rope(k_ref[...], gk_ref[...], cos2, sin2, gmat)
    v = v_ref[...]
    ck = ck_ref[...]
    cv = cv_ref[...]
    k3 = k.reshape(bb, s, KV_WIDTH)
    v3 = v.reshape(bb, s, KV_WIDTH)
    ks_ref[:, :WINDOW - s, :] = ck[:, s:, :]
    ks_ref[:, WINDOW - s:, :] = k3
    vs_ref[:, :WINDOW - s, :] = cv[:, s:, :]
    vs_ref[:, WINDOW - s:, :] = v3

    q3 = q.astype(BF16).reshape(bb, s, ATTN_WIDTH)
    ckb = ck.astype(BF16)
    cvb = cv.astype(BF16)
    k3b = k3.astype(BF16)
    v3b = v3.astype(BF16)
    rows = GQA * s
    qi = lax.broadcasted_iota(jnp.int32, (bb, rows, WINDOW), 1) & (s - 1)
    tc = lax.broadcasted_iota(jnp.int32, (bb, rows, WINDOW), 2)
    mask_c = tc > qi
    qi2 = lax.broadcasted_iota(jnp.int32, (bb, rows, s), 1) & (s - 1)
    tn = lax.broadcasted_iota(jnp.int32, (bb, rows, s), 2)
    mask_n = tn <= qi2

    pieces = []
    for g in range(N_KV_HEADS):
        qg = jnp.concatenate(
            [q3[:, :, (g * GQA + h) * HEAD_DIM:(g * GQA + h + 1) * HEAD_DIM] for h in range(GQA)], axis=1)
        sl = slice(g * HEAD_DIM, (g + 1) * HEAD_DIM)
        lc = jnp.einsum("bqd,bkd->bqk", qg, ckb[:, :, sl], preferred_element_type=F32)
        ln = jnp.einsum("bqd,bkd->bqk", qg, k3b[:, :, sl], preferred_element_type=F32)
        lc = jnp.where(mask_c, lc, NEG_INF)
        ln = jnp.where(mask_n, ln, NEG_INF)
        sink = sink_ref[g]
        m = jnp.maximum(jnp.maximum(jnp.max(lc, axis=-1, keepdims=True), jnp.max(ln, axis=-1, keepdims=True)), sink)
        ec = jnp.exp(lc - m)
        en = jnp.exp(ln - m)
        den = jnp.sum(ec, axis=-1, keepdims=True) + jnp.sum(en, axis=-1, keepdims=True) + jnp.exp(sink - m)
        og = (jnp.einsum("bqk,bkd->bqd", (ec / den).astype(BF16), cvb[:, :, sl], preferred_element_type=F32)
              + jnp.einsum("bqk,bkd->bqd", (en / den).astype(BF16), v3b[:, :, sl], preferred_element_type=F32))
        pieces.extend(og[:, h * s:(h + 1) * s, :] for h in range(GQA))
    ao = jnp.concatenate(pieces, axis=2).reshape(bb * s, ATTN_WIDTH)
    ao_ref[...] = _rms_rows(ao, gao_ref[...]).astype(BF16)


def _attn_sample(q, k, v, ck, cv, cos2, sin2, gq, gk, sink_col, gao, gmat, batch, s, bb):
    rows = bb * s
    row = lambda i: (i, 0)
    const2 = lambda i: (0, 0)
    b3 = lambda i: (i, 0, 0)
    return pl.pallas_call(
        functools.partial(_attn_sample_body, bb=bb, s=s),
        grid=(batch // bb,),
        in_specs=[
            pl.BlockSpec((rows, ATTN_WIDTH), row),
            pl.BlockSpec((rows, KV_WIDTH), row),
            pl.BlockSpec((rows, KV_WIDTH), row),
            pl.BlockSpec((bb, WINDOW, KV_WIDTH), b3),
            pl.BlockSpec((bb, WINDOW, KV_WIDTH), b3),
            pl.BlockSpec((rows, LANES), const2),
            pl.BlockSpec((rows, LANES), const2),
            pl.BlockSpec((1, ATTN_WIDTH), const2),
            pl.BlockSpec((1, KV_WIDTH), const2),
            pl.BlockSpec((N_KV_HEADS, GQA * s, 1), lambda i: (0, 0, 0)),
            pl.BlockSpec((1, ATTN_WIDTH), const2),
            pl.BlockSpec((MXU_DIM, MXU_DIM), const2),
        ],
        out_specs=[
            pl.BlockSpec((rows, ATTN_WIDTH), row),
            pl.BlockSpec((bb, WINDOW, KV_WIDTH), b3),
            pl.BlockSpec((bb, WINDOW, KV_WIDTH), b3),
        ],
        out_shape=[
            jax.ShapeDtypeStruct((batch * s, ATTN_WIDTH), BF16),
            jax.ShapeDtypeStruct((batch, WINDOW, KV_WIDTH), F32),
            jax.ShapeDtypeStruct((batch, WINDOW, KV_WIDTH), F32),
        ],
        compiler_params=_cparams(("arbitrary",)),
        name="attn_sample",
    )(q, k, v, ck, cv, cos2, sin2, gq, gk, sink_col, gao, gmat)


def _lru_gates(xc, wa_ref, wx_ref, ba, bx, lam):
    xb = xc.astype(BF16)
    r_cols, i_cols = [], []
    for c in range(LRU_WIDTH // MXU_DIM):
        sl = slice(c * MXU_DIM, (c + 1) * MXU_DIM)
        r_cols.append(jnp.dot(xb[:, sl], wa_ref[c], preferred_element_type=F32))
        i_cols.append(jnp.dot(xb[:, sl], wx_ref[c], preferred_element_type=F32))
    r = 1.0 / (1.0 + jnp.exp(-(jnp.concatenate(r_cols, axis=1) + ba)))
    ig = 1.0 / (1.0 + jnp.exp(-(jnp.concatenate(i_cols, axis=1) + bx)))
    neg_lam = -lam
    softplus = jnp.maximum(neg_lam, 0.0) + jnp.log1p(jnp.exp(-jnp.abs(neg_lam)))
    log_a = -LRU_C * r * softplus
    a = jnp.exp(log_a)
    one_minus_a2 = -jnp.tanh(log_a) * (a * a + 1.0)
    root = jnp.where(one_minus_a2 > 0.0, one_minus_a2 * lax.rsqrt(one_minus_a2), 0.0)
    u = root * (ig * xc)
    return a, u


def _segment_scan(a, u):
    rows, width = a.shape
    a = a.reshape(rows // SUBLANES, SUBLANES, width)
    u = u.reshape(rows // SUBLANES, SUBLANES, width)
    t = lax.broadcasted_iota(jnp.int32, a.shape, 1)
    d = 1
    while d < SUBLANES:
        valid = t >= d
        u = jnp.where(valid, a * pltpu.roll(u, d, axis=1) + u, u)
        a = jnp.where(valid, a * pltpu.roll(a, d, axis=1), a)
        d *= 2
    return a.reshape(rows, width), u.reshape(rows, width)


def _lru_prompt_body(xr_ref, yg_ref, cw_ref, cb_ref, wa_ref, wx_ref, ba_ref, bx_ref, lam_ref, glo_ref,
                     *refs, tb, with_transpose):
    if with_transpose:
        src_ref, lo_ref, hlast_ref, dst_ref, xpad, hcarry = refs
        dst_ref[...] = src_ref[...].T.astype(BF16)
    else:
        lo_ref, hlast_ref, xpad, hcarry = refs
    t_blk = pl.program_id(1)

    @pl.when(t_blk == 0)
    def _():
        xpad[:SUBLANES, :] = jnp.zeros((SUBLANES, LRU_WIDTH), F32)
        hcarry[...] = jnp.zeros_like(hcarry)

    xr = xr_ref[...]
    xpad[SUBLANES:, :] = xr
    xc = xr * cw_ref[CONV_W - 1:CONV_W, :] + cb_ref[...]
    for kshift in range(1, CONV_W):
        shifted = xpad[SUBLANES - kshift:SUBLANES - kshift + tb, :]
        xc = xc + shifted * cw_ref[CONV_W - 1 - kshift:CONV_W - kshift, :]
    a, u = _lru_gates(xc, wa_ref, wx_ref, ba_ref[...], bx_ref[...], lam_ref[...])
    acum, hloc = _segment_scan(a, u)
    tiles = []
    carry = hcarry[...]
    for r in range(tb // SUBLANES):
        rows = slice(r * SUBLANES, (r + 1) * SUBLANES)
        h_tile = hloc[rows] + acum[rows] * carry
        tiles.append(h_tile)
        carry = h_tile[SUBLANES - 1:SUBLANES]
    h = jnp.concatenate(tiles, axis=0)
    y = h * jax.nn.gelu(yg_ref[...])
    lo_ref[...] = _rms_rows(y, glo_ref[...]).astype(BF16)
    hlast = h[tb - 1:tb, :]
    hcarry[...] = hlast
    xpad[:SUBLANES, :] = xr[tb - SUBLANES:, :]
    hlast_ref[0] = hlast


def _lru_prompt(xr, yg, cw, cb, wa_bd, wx_bd, ba, bx, lam, glo, batch, seq, tb, transpose_src=None):
    nt = seq // tb
    row = lambda b, t: (b * nt + t, 0)
    const2 = lambda b, t: (0, 0)
    const3 = lambda b, t: (0, 0, 0)
    nchunk = LRU_WIDTH // MXU_DIM
    in_specs = [
        pl.BlockSpec((tb, LRU_WIDTH), row),
        pl.BlockSpec((tb, LRU_WIDTH), row),
        pl.BlockSpec((CONV_W, LRU_WIDTH), const2),
        pl.BlockSpec((1, LRU_WIDTH), const2),
        pl.BlockSpec((nchunk, MXU_DIM, MXU_DIM), const3),
        pl.BlockSpec((nchunk, MXU_DIM, MXU_DIM), const3),
        pl.BlockSpec((1, LRU_WIDTH), const2),
        pl.BlockSpec((1, LRU_WIDTH), const2),
        pl.BlockSpec((1, LRU_WIDTH), const2),
        pl.BlockSpec((1, LRU_WIDTH), const2),
    ]
    out_specs = [
        pl.BlockSpec((tb, LRU_WIDTH), row),
        pl.BlockSpec((1, 1, LRU_WIDTH), lambda b, t: (b, 0, 0)),
    ]
    out_shape = [
        jax.ShapeDtypeStruct((batch * seq, LRU_WIDTH), BF16),
        jax.ShapeDtypeStruct((batch, 1, LRU_WIDTH), F32),
    ]
    args = [xr, yg, cw, cb, wa_bd, wx_bd, ba, bx, lam, glo]
    if transpose_src is not None:
        rows, cols = transpose_src.shape
        slab = rows // (batch * nt)
        in_specs.append(pl.BlockSpec((slab, cols), row))
        out_specs.append(pl.BlockSpec((cols, slab), lambda b, t: (0, b * nt + t)))
        out_shape.append(jax.ShapeDtypeStruct((cols, rows), BF16))
        args.append(transpose_src)
    return pl.pallas_call(
        functools.partial(_lru_prompt_body, tb=tb, with_transpose=transpose_src is not None),
        grid=(batch, nt),
        in_specs=in_specs,
        out_specs=out_specs,
        out_shape=out_shape,
        scratch_shapes=[pltpu.VMEM((SUBLANES + tb, LRU_WIDTH), F32), pltpu.VMEM((1, LRU_WIDTH), F32)],
        compiler_params=_cparams(("arbitrary", "arbitrary")),
        name="lru_prompt",
    )(*args)


def _lru_sample_body(xr_ref, yg_ref, cpad_ref, h0_ref, cw_ref, cb_ref, wa_ref, wx_ref, ba_ref, bx_ref, lam_ref,
                     glo_ref, lo_ref, h_ref, *, s):
    xr = xr_ref[...]
    cpad = cpad_ref[...]
    rows = xr.shape[0]
    t = lax.broadcasted_iota(jnp.int32, xr.shape, 0) & (s - 1)
    xc = xr * cw_ref[CONV_W - 1:CONV_W, :] + cb_ref[...]
    for kshift in range(1, CONV_W):
        shifted = jnp.where(t < kshift, pltpu.roll(cpad, rows - s + kshift, axis=0), pltpu.roll(xr, kshift, axis=0))
        xc = xc + shifted * cw_ref[CONV_W - 1 - kshift:CONV_W - kshift, :]
    a, u = _lru_gates(xc, wa_ref, wx_ref, ba_ref[...], bx_ref[...], lam_ref[...])
    assert s == SUBLANES
    acum, hloc = _segment_scan(a, u)
    h = hloc + acum * h0_ref[...]
    h_ref[...] = h
    y = h * jax.nn.gelu(yg_ref[...])
    lo_ref[...] = _rms_rows(y, glo_ref[...]).astype(BF16)


def _lru_sample(xr, yg, cpad, h0rows, cw, cb, wa_bd, wx_bd, ba, bx, lam, glo, s, tb):
    n = xr.shape[0]
    row = lambda i: (i, 0)
    const2 = lambda i: (0, 0)
    const3 = lambda i: (0, 0, 0)
    nchunk = LRU_WIDTH // MXU_DIM
    return pl.pallas_call(
        functools.partial(_lru_sample_body, s=s),
        grid=(n // tb,),
        in_specs=[
            pl.BlockSpec((tb, LRU_WIDTH), row),
            pl.BlockSpec((tb, LRU_WIDTH), row),
            pl.BlockSpec((tb, LRU_WIDTH), row),
            pl.BlockSpec((tb, LRU_WIDTH), row),
            pl.BlockSpec((CONV_W, LRU_WIDTH), const2),
            pl.BlockSpec((1, LRU_WIDTH), const2),
            pl.BlockSpec((nchunk, MXU_DIM, MXU_DIM), const3),
            pl.BlockSpec((nchunk, MXU_DIM, MXU_DIM), const3),
            pl.BlockSpec((1, LRU_WIDTH), const2),
            pl.BlockSpec((1, LRU_WIDTH), const2),
            pl.BlockSpec((1, LRU_WIDTH), const2),
            pl.BlockSpec((1, LRU_WIDTH), const2),
        ],
        out_specs=[pl.BlockSpec((tb, LRU_WIDTH), row), pl.BlockSpec((tb, LRU_WIDTH), row)],
        out_shape=[jax.ShapeDtypeStruct((n, LRU_WIDTH), BF16), jax.ShapeDtypeStruct((n, LRU_WIDTH), F32)],
        compiler_params=_cparams(("arbitrary",)),
        name="lru_sample",
    )(xr, yg, cpad, h0rows, cw, cb, wa_bd, wx_bd, ba, bx, lam, glo)


def _outproj_body(x_ref, ao_ref, lo_ref, w_ref, g_ref, x1_ref, hnt_ref):
    mixed = (jnp.dot(ao_ref[...], w_ref[:ATTN_WIDTH, :], preferred_element_type=F32)
             + jnp.dot(lo_ref[...], w_ref[ATTN_WIDTH:, :], preferred_element_type=F32))
    x1 = x_ref[...] + mixed
    x1_ref[...] = x1
    hnt_ref[...] = _rms_rows(x1, g_ref[...]).T.astype(BF16)


def _outproj(x, ao, lo, w_bf, g, tm):
    n = x.shape[0]
    row = lambda i: (i, 0)
    const2 = lambda i: (0, 0)
    return pl.pallas_call(
        _outproj_body,
        grid=(n // tm,),
        in_specs=[
            pl.BlockSpec((tm, D_MODEL), row),
            pl.BlockSpec((tm, ATTN_WIDTH), row),
            pl.BlockSpec((tm, LRU_WIDTH), row),
            pl.BlockSpec((D_MODEL, D_MODEL), const2),
            pl.BlockSpec((1, D_MODEL), const2),
        ],
        out_specs=[pl.BlockSpec((tm, D_MODEL), row), pl.BlockSpec((D_MODEL, tm), lambda i: (0, i))],
        out_shape=[jax.ShapeDtypeStruct((n, D_MODEL), F32), jax.ShapeDtypeStruct((D_MODEL, n), BF16)],
        compiler_params=_cparams(("arbitrary",)),
        name="outproj",
    )(x, ao, lo, w_bf, g)


PEER_CAND_COLS = tuple(min(PEER_TOPK, (PEER_TOPK + 1) // (a + 1)) for a in range(PEER_TOPK))
ROUTE_LANES = 128
ROUTE_TILES_PER_STEP = 2


def _sort_network(n):
    pairs, p = [], 1
    while p < n:
        k = p
        while k >= 1:
            for j in range(k % p, n - k, 2 * k):
                for i in range(min(k, n - j - k)):
                    if (i + j) // (2 * p) == (i + j + k) // (2 * p):
                        pairs.append((i + j, i + j + k))
            k //= 2
        p *= 2
    return tuple(pairs)


SORT16 = _sort_network(N_KEYS // SUBLANES)


def _pop_lists(heads_and_ids, nvals):
    levels = [list(lv) for lv, _ in heads_and_ids]
    ids = [i for _, i in heads_and_ids]
    big = float(SUBLANES * len(levels))
    vals = []
    for r in range(nvals):
        top = levels[0][0]
        for lv in levels[1:]:
            top = jnp.maximum(top, lv[0])
        m = jnp.max(top, axis=0, keepdims=True)
        vals.append(m)
        if r == nvals - 1:
            break
        first = None
        for lv, i in zip(levels, ids):
            cand = jnp.where(lv[0] == m, i, big)
            first = cand if first is None else jnp.minimum(first, cand)
        first = jnp.min(first, axis=0, keepdims=True)
        for lv, i in zip(levels, ids):
            hit = i == first
            live = min(len(lv), nvals - 1 - r)
            for d in range(live):
                below = lv[d + 1] if d + 1 < len(lv) else -jnp.inf
                lv[d] = jnp.where(hit, below, lv[d])
    return vals


def _top16(s, ltri, ones8):
    lanes = s.shape[1]
    lv = [s[SUBLANES * r:SUBLANES * (r + 1)] for r in range(N_KEYS // SUBLANES)]
    for i, j in SORT16:
        lv[i], lv[j] = jnp.maximum(lv[i], lv[j]), jnp.minimum(lv[i], lv[j])
    subf = lax.broadcasted_iota(jnp.int32, (SUBLANES, lanes), 0).astype(F32)
    vals = _pop_lists([(lv, subf)], PEER_TOPK)
    v16 = vals[PEER_TOPK - 1]
    gt = s > v16
    eq = s == v16
    n_gt = jnp.dot(ones8, jnp.where(gt, 1.0, 0.0).astype(BF16), preferred_element_type=F32)[0:1]
    eq_before = jnp.dot(ltri, jnp.where(eq, 1.0, 0.0).astype(BF16), preferred_element_type=F32)
    need = float(PEER_TOPK) - n_gt
    kept = jnp.where(gt, s, jnp.where(eq, jnp.where(eq_before < need, s, -jnp.inf), -jnp.inf))
    return vals, kept


def _best_sums(v0, v1):
    lanes = v0[0].shape[1]
    lo0 = jnp.concatenate(v0[:SUBLANES], axis=0)
    hi0 = jnp.concatenate(v0[SUBLANES:], axis=0)
    row8 = lax.broadcasted_iota(jnp.int32, (SUBLANES, lanes), 0)
    subf = row8.astype(F32)
    levels = []
    for b in range(PEER_TOPK):
        c = lo0 + v1[b]
        nrows = sum(1 for a in range(SUBLANES) if PEER_CAND_COLS[a] > b)
        levels.append(c if nrows == SUBLANES else jnp.where(row8 < nrows, c, -jnp.inf))
    return _pop_lists([(levels, subf), ([hi0 + v1[0]], subf + float(SUBLANES))], PEER_TOPK + 1)


def _route_body(hnt_ref, wq_ref, keys_ref, ltri_ref, ones_ref, s1m_ref, e1z_ref, t0_ref, e0_ref, s_scr, *, tm):
    qt = jnp.dot(wq_ref[...], hnt_ref[...], preferred_element_type=F32)
    half = D_KEY // 2
    for hc in range(2 * PEER_HEADS):
        qhc = qt[hc * half:(hc + 1) * half, :].astype(BF16)
        s_scr[hc] = jnp.dot(keys_ref[hc], qhc, preferred_element_type=F32)

    def route_tile(h, lanes):
        ltri = ltri_ref[...]
        ones8 = ones_ref[...]
        v0, s0m = _top16(s_scr[2 * h, :, lanes], ltri, ones8)
        v1, s1m = _top16(s_scr[2 * h + 1, :, lanes], ltri, ones8)
        best = _best_sums(v0, v1)
        mx = best[0]
        z = jnp.exp(best[0] - mx)
        for r in range(1, PEER_TOPK):
            z = z + jnp.exp(best[r] - mx)
        thr = 0.5 * (best[PEER_TOPK - 1] + best[PEER_TOPK])
        s1m_ref[h, :, lanes] = s1m
        e1z_ref[h, :, lanes] = jnp.exp(s1m - v1[0]) / z
        t0_ref[h, :, lanes] = thr - s0m
        e0_ref[h, :, lanes] = jnp.exp(s0m - v0[0])

    per_head = tm // (ROUTE_TILES_PER_STEP * ROUTE_LANES)

    def step(it, carry):
        h = it // per_head
        c = it % per_head
        for t in range(ROUTE_TILES_PER_STEP):
            start = (c * ROUTE_TILES_PER_STEP + t) * ROUTE_LANES
            route_tile(h, pl.ds(pl.multiple_of(start, ROUTE_LANES), ROUTE_LANES))
        return carry

    lax.fori_loop(0, PEER_HEADS * per_head, step, 0)


def _route(hnt, wq_t, keys, tm):
    n = hnt.shape[1]
    blk = lambda i: (0, 0, i)
    shape = jax.ShapeDtypeStruct((PEER_HEADS, N_KEYS, n), F32)
    spec = pl.BlockSpec((PEER_HEADS, N_KEYS, tm), blk)
    ltri = jnp.asarray(np.tril(np.ones((N_KEYS, N_KEYS)), -1), BF16)
    ones8 = jnp.ones((SUBLANES, N_KEYS), BF16)
    return pl.pallas_call(
        functools.partial(_route_body, tm=tm),
        grid=(n // tm,),
        in_specs=[
            pl.BlockSpec((D_MODEL, tm), lambda i: (0, i)),
            pl.BlockSpec((PEER_HEADS * D_KEY, D_MODEL), lambda i: (0, 0)),
            pl.BlockSpec((2 * PEER_HEADS, N_KEYS, D_KEY // 2), lambda i: (0, 0, 0)),
            pl.BlockSpec((N_KEYS, N_KEYS), lambda i: (0, 0)),
            pl.BlockSpec((SUBLANES, N_KEYS), lambda i: (0, 0)),
        ],
        out_specs=[spec, spec, spec, spec],
        out_shape=[shape, shape, shape, shape],
        scratch_shapes=[pltpu.VMEM((2 * PEER_HEADS, N_KEYS, tm), F32)],
        compiler_params=_cparams(("arbitrary",)),
        name="peer_route",
    )(hnt, wq_t, keys, ltri, ones8)


EXPERT_LANES = 256
EXPERT_BLOCK = SUBLANES * N_KEYS
EXPERT_ROWS = EXPERT_BLOCK // N_KEYS
U_KCHUNK = 512
U_NCHUNK = D_MODEL // U_KCHUNK
V_KCHUNK = 512
V_NCHUNK = EXPERT_BLOCK // V_KCHUNK


def _routing_weight(s1m_ref, e1z_ref, t0_ref, e0_ref, k, ii, lanes):
    width = lanes.stop - lanes.start
    trow = [jnp.broadcast_to(t0_ref[h, k, ii:ii + 1, lanes], (SUBLANES, width)) for h in range(PEER_HEADS)]
    e0row = [jnp.broadcast_to(e0_ref[h, k, ii:ii + 1, lanes], (SUBLANES, width)) for h in range(PEER_HEADS)]
    out = []
    for j in range(0, N_KEYS, SUBLANES):
        w = None
        for h in range(PEER_HEADS):
            term = jnp.where(s1m_ref[h, j:j + SUBLANES, lanes] >= trow[h], e1z_ref[h, j:j + SUBLANES, lanes], 0.0) * e0row[h]
            w = term if w is None else w + term
        out.append(w)
    return jnp.concatenate(out, axis=0)


def _experts_body(hnt_ref, *refs, tm, nk):
    u_refs = refs[:U_NCHUNK]
    vt_refs = refs[U_NCHUNK:U_NCHUNK + V_NCHUNK]
    (s1m_ref, e1z_ref, t0_ref, e0_ref, x1_ref, y_ref, acc, act, wbuf, pt_a, pt_b) = refs[U_NCHUNK + V_NCHUNK:]
    k = pl.program_id(1)
    kt = jnp.minimum(k, nk - 1)
    nlane = tm // EXPERT_LANES

    def weights():
        for ii in range(EXPERT_ROWS):
            for c in range(nlane):
                rows = slice(ii * N_KEYS, (ii + 1) * N_KEYS)
                lanes = slice(c * EXPERT_LANES, (c + 1) * EXPERT_LANES)
                wbuf[rows, lanes] = _routing_weight(s1m_ref, e1z_ref, t0_ref, e0_ref, kt, ii, lanes)

    def dot1():
        for c in range(nlane):
            lanes = slice(c * EXPERT_LANES, (c + 1) * EXPERT_LANES)
            a = None
            for kc in range(U_NCHUNK):
                d = jnp.dot(u_refs[kc][...], hnt_ref[kc * U_KCHUNK:(kc + 1) * U_KCHUNK, lanes],
                            preferred_element_type=F32)
                a = d if a is None else a + d
            act[:, lanes] = a

    def weighted_act(pt):
        for ii in range(EXPERT_ROWS):
            rows = slice(ii * N_KEYS, (ii + 1) * N_KEYS)
            for c in range(nlane):
                lanes = slice(c * EXPERT_LANES, (c + 1) * EXPERT_LANES)
                pt[rows, lanes] = (wbuf[rows, lanes] * jax.nn.gelu(act[rows, lanes])).astype(BF16)

    def dot2(pt):
        a = None
        for c in range(V_NCHUNK):
            d = jnp.dot(vt_refs[c][...], pt[c * V_KCHUNK:(c + 1) * V_KCHUNK, :], preferred_element_type=F32)
            a = d if a is None else a + d
        acc[...] += a

    @pl.when(k == 0)
    def _():
        acc[...] = jnp.zeros_like(acc)
        pt_b[...] = jnp.zeros_like(pt_b)

    @pl.when(k < nk)
    def _():
        weights()
        dot1()

    def phase2(pt_new, pt_old):
        weighted_act(pt_new)
        dot2(pt_old)

    @pl.when(k % 2 == 0)
    def _():
        phase2(pt_a, pt_b)

    @pl.when(k % 2 == 1)
    def _():
        phase2(pt_b, pt_a)

    @pl.when(k == nk)
    def _():
        y_ref[...] = x1_ref[...] + acc[...].T


def _experts(hnt, u_bf, vt_bf, s1m, e1z, t0, e0, x1, tm):
    n = hnt.shape[1]
    nk = N_EXPERTS // EXPERT_BLOCK
    assert nk % 2 == 0
    tok = lambda t, k: (t, 0)
    once = pl.Buffered(1)
    small = pl.BlockSpec((PEER_HEADS, N_KEYS, tm), lambda t, k: (0, 0, t), pipeline_mode=once)
    tiled = pl.BlockSpec((PEER_HEADS, N_KEYS // SUBLANES, SUBLANES, tm), lambda t, k: (0, 0, 0, t), pipeline_mode=once)
    t0 = t0.reshape(PEER_HEADS, N_KEYS // SUBLANES, SUBLANES, n)
    e0 = e0.reshape(PEER_HEADS, N_KEYS // SUBLANES, SUBLANES, n)
    u_specs = [pl.BlockSpec((EXPERT_BLOCK, U_KCHUNK), lambda t, k, kc=kc: (jnp.minimum(k, nk - 1), kc))
               for kc in range(U_NCHUNK)]
    vt_specs = [pl.BlockSpec((D_MODEL, V_KCHUNK), lambda t, k, c=c: (0, V_NCHUNK * jnp.maximum(k - 1, 0) + c))
                for c in range(V_NCHUNK)]
    return pl.pallas_call(
        functools.partial(_experts_body, tm=tm, nk=nk),
        grid=(n // tm, nk + 1),
        in_specs=[
            pl.BlockSpec((D_MODEL, tm), lambda t, k: (0, t), pipeline_mode=once),
            *u_specs,
            *vt_specs,
            small, small, tiled, tiled,
            pl.BlockSpec((tm, D_MODEL), tok),
        ],
        out_specs=pl.BlockSpec((tm, D_MODEL), tok),
        out_shape=jax.ShapeDtypeStruct((n, D_MODEL), F32),
        scratch_shapes=[
            pltpu.VMEM((D_MODEL, tm), F32),
            pltpu.VMEM((EXPERT_BLOCK, tm), F32),
            pltpu.VMEM((EXPERT_BLOCK, tm), F32),
            pltpu.VMEM((EXPERT_BLOCK, tm), BF16),
            pltpu.VMEM((EXPERT_BLOCK, tm), BF16),
        ],
        compiler_params=_cparams(("arbitrary", "arbitrary")),
        name="peer_experts",
    )(hnt, *([u_bf] * len(u_specs)), *([vt_bf] * len(vt_specs)), s1m, e1z, t0, e0, x1)


def _rope_tables(pos):
    half = HEAD_DIM // 2
    inv = jnp.exp(-math.log(ROPE_THETA) * jnp.arange(half, dtype=F32) * (2.0 / HEAD_DIM))
    ang = pos[:, None] * inv[None, :]
    cos, sin = jnp.cos(ang), jnp.sin(ang)
    cos2 = jnp.concatenate([cos, cos, cos, cos], axis=1)
    sin2 = jnp.concatenate([-sin, sin, -sin, sin], axis=1)
    return cos2, sin2


def _block_diag(w):
    per = MXU_DIM // LRU_BLOCK
    w4 = w.reshape(LRU_WIDTH // MXU_DIM, per, LRU_BLOCK, LRU_BLOCK)
    eye = jnp.eye(per, dtype=w.dtype)
    return jnp.einsum("cpde,pq->cpdqe", w4, eye).reshape(LRU_WIDTH // MXU_DIM, MXU_DIM, MXU_DIM).astype(BF16)


def kernel(x_prompt, x_sample, cache_k, cache_v, state_conv, state_h, norm_mix_g, w_in, q_norm_g, k_norm_g, attn_sinks, conv_w, conv_b, w_rec_gate, b_rec_gate, w_in_gate, b_in_gate, lru_lambda, attn_out_g, lru_out_g, w_out, norm_ffn_g, w_peer_q, peer_sub_keys, peer_u, peer_v):
    depth = w_in.shape[0]
    assert depth == 1
    batch, seq, _ = x_prompt.shape
    dbatch, dseq, _ = x_sample.shape
    past_len = 8192
    l = 0

    w_in_bf = w_in[l].astype(BF16)
    w_out_bf = w_out[l].astype(BF16)
    wq_t = w_peer_q[l].T.astype(BF16)
    keys = peer_sub_keys[l].reshape(2 * PEER_HEADS, N_KEYS, D_KEY // 2).astype(BF16)
    wa_bd = _block_diag(w_rec_gate[l])
    wx_bd = _block_diag(w_in_gate[l])
    ba = b_rec_gate[l].reshape(1, LRU_WIDTH)
    bx = b_in_gate[l].reshape(1, LRU_WIDTH)
    lam = lru_lambda[l].reshape(1, LRU_WIDTH)
    cw = conv_w[l]
    cb = conv_b[l].reshape(1, LRU_WIDTH)
    g_mix = norm_mix_g[l].reshape(1, D_MODEL)
    g_ffn = norm_ffn_g[l].reshape(1, D_MODEL)
    gq = jnp.tile(q_norm_g[l], N_Q_HEADS).reshape(1, ATTN_WIDTH)
    gk = jnp.tile(k_norm_g[l], N_KV_HEADS).reshape(1, KV_WIDTH)
    gao = attn_out_g[l].reshape(1, ATTN_WIDTH)
    glo = lru_out_g[l].reshape(1, LRU_WIDTH)
    gmat = jnp.asarray(np.kron(np.eye(MXU_DIM // HEAD_DIM), np.full((HEAD_DIM, HEAD_DIM), 1.0 / HEAD_DIM)), BF16)
    sinks = attn_sinks[l].astype(F32).reshape(N_KV_HEADS, GQA)
    sink_p = jnp.repeat(sinks, WINDOW, axis=1).reshape(N_KV_HEADS, GQA * WINDOW, 1)
    sink_s = jnp.repeat(sinks, dseq, axis=1).reshape(N_KV_HEADS, GQA * dseq, 1)
    cos_p, sin_p = _rope_tables(jnp.arange(seq, dtype=F32))
    cos_s, sin_s = _rope_tables(float(past_len) + jnp.arange(dseq, dtype=F32))

    n_p = batch * seq
    n_s = dbatch * dseq
    xp2 = x_prompt.reshape(n_p, D_MODEL)
    xs2 = x_sample.reshape(n_s, D_MODEL)

    qp, kp, vp, xrp, ygp, u_bf = _inproj(xp2, g_mix, w_in_bf, 256, cast_src=peer_u[l])
    ao_p, k_last = _attn_prompt(qp, kp, vp, cos_p, sin_p, gq, gk, sink_p, gao, gmat, batch, seq)
    lo_p, h_p, vt_bf = _lru_prompt(xrp, ygp, cw, cb, wa_bd, wx_bd, ba, bx, lam, glo, batch, seq, 256,
                                   transpose_src=peer_v[l])
    x1p, hn_p = _outproj(xp2, ao_p, lo_p, w_out_bf, g_ffn, 256)

    bb = 8
    qs, ks, vs, xrs, ygs = _inproj(xs2, g_mix, w_in_bf, 256)
    cos_sb = jnp.tile(cos_s, (bb, 1))
    sin_sb = jnp.tile(sin_s, (bb, 1))
    ck = cache_k[l].reshape(dbatch, WINDOW, KV_WIDTH)
    cv = cache_v[l].reshape(dbatch, WINDOW, KV_WIDTH)
    ao_s, k_s, v_s = _attn_sample(qs, ks, vs, ck, cv, cos_sb, sin_sb, gq, gk, sink_s, gao, gmat, dbatch, dseq, bb)
    cpad = jnp.pad(state_conv[l], ((0, 0), (dseq - (CONV_W - 1), 0), (0, 0))).reshape(n_s, LRU_WIDTH)
    h0rows = jnp.repeat(state_h[l], dseq, axis=0)
    lo_s, h_s = _lru_sample(xrs, ygs, cpad, h0rows, cw, cb, wa_bd, wx_bd, ba, bx, lam, glo, dseq, 256)
    x1s, hn_s = _outproj(xs2, ao_s, lo_s, w_out_bf, g_ffn, 256)

    tm = 512
    outs = []
    for x1, hn in ((x1p, hn_p), (x1s, hn_s)):
        s1m, e1z, t0, e0 = _route(hn, wq_t, keys, tm)
        outs.append(_experts(hn, u_bf, vt_bf, s1m, e1z, t0, e0, x1, tm))
    y_p = outs[0].reshape(batch, seq, D_MODEL)
    y_s = outs[1].reshape(dbatch, dseq, D_MODEL)

    kv_shape = (N_KV_HEADS, HEAD_DIM)
    k_prompt = k_last.reshape(1, batch, WINDOW, *kv_shape)
    v_prompt = vp.reshape(batch, seq, *kv_shape)[:, -WINDOW:][None]
    conv_prompt = xrp.reshape(batch, seq, LRU_WIDTH)[:, -(CONV_W - 1):][None]
    h_prompt = h_p.reshape(1, batch, LRU_WIDTH)
    k_sample = k_s.reshape(1, dbatch, WINDOW, *kv_shape)
    v_sample = v_s.reshape(1, dbatch, WINDOW, *kv_shape)
    conv_sample = xrs.reshape(dbatch, dseq, LRU_WIDTH)[:, -(CONV_W - 1):][None]
    h_sample = h_s.reshape(dbatch, dseq, LRU_WIDTH)[:, -1][None]
    return (y_p, y_s, k_prompt, v_prompt, conv_prompt, h_prompt, k_sample, v_sample, conv_sample, h_sample)
```

```python
import functools
import math

import jax
import jax.numpy as jnp
import numpy as np
from jax import lax
from jax.experimental import pallas as pl
from jax.experimental.pallas import tpu as pltpu

F32 = jnp.float32
BF16 = jnp.bfloat16

D_MODEL = 2048
HEAD_DIM = 64
N_Q_HEADS = 16
N_KV_HEADS = 4
GQA = N_Q_HEADS // N_KV_HEADS
ATTN_WIDTH = N_Q_HEADS * HEAD_DIM
KV_WIDTH = N_KV_HEADS * HEAD_DIM
WINDOW = 128
ROPE_THETA = 10000.0
LRU_WIDTH = D_MODEL - ATTN_WIDTH
LRU_BLOCK = 64
CONV_W = 4
LRU_C = 8.0
N_KEYS = 128
N_EXPERTS = N_KEYS * N_KEYS
PEER_HEADS = 8
PEER_TOPK = 16
D_KEY = 256
EPS = 1e-6
NEG_INF = -1e30

MXU_DIM = 256
LANES = 128
SUBLANES = 8
VMEM_LIMIT_BYTES = 60 * 1024 * 1024


def _cparams(sem, flags=None):
    return pltpu.CompilerParams(dimension_semantics=sem, vmem_limit_bytes=VMEM_LIMIT_BYTES, flags=flags)


IN_CUTS = (0, ATTN_WIDTH, ATTN_WIDTH + KV_WIDTH, ATTN_WIDTH + 2 * KV_WIDTH,
           ATTN_WIDTH + 2 * KV_WIDTH + LRU_WIDTH, ATTN_WIDTH + 2 * KV_WIDTH + 2 * LRU_WIDTH)


def _inproj_body(x_ref, g_ref, w_ref, *refs, with_cast):
    if with_cast:
        src_ref, q_ref, k_ref, v_ref, xr_ref, yg_ref, dst_ref = refs
        dst_ref[...] = src_ref[...].astype(BF16)
    else:
        q_ref, k_ref, v_ref, xr_ref, yg_ref = refs
    x = x_ref[...]
    ms = jnp.mean(x * x, axis=-1, keepdims=True)
    hn = (x * lax.rsqrt(ms + EPS) * g_ref[...]).astype(BF16)
    for o_ref, lo, hi in zip((q_ref, k_ref, v_ref, xr_ref, yg_ref), IN_CUTS[:-1], IN_CUTS[1:]):
        o_ref[...] = jnp.dot(hn, w_ref[:, lo:hi], preferred_element_type=F32)


def _inproj(x, g, w_bf, tm, cast_src=None):
    n = x.shape[0]
    steps = n // tm
    widths = [hi - lo for lo, hi in zip(IN_CUTS[:-1], IN_CUTS[1:])]
    in_specs = [
        pl.BlockSpec((tm, D_MODEL), lambda i: (i, 0)),
        pl.BlockSpec((1, D_MODEL), lambda i: (0, 0)),
        pl.BlockSpec((D_MODEL, IN_CUTS[-1]), lambda i: (0, 0), pipeline_mode=pl.Buffered(1)),
    ]
    out_specs = [pl.BlockSpec((tm, w), lambda i: (i, 0)) for w in widths]
    out_shape = [jax.ShapeDtypeStruct((n, w), F32) for w in widths]
    args = [x, g, w_bf]
    if cast_src is not None:
        rows, cols = cast_src.shape
        slab = pl.BlockSpec((rows // steps, cols), lambda i: (i, 0))
        in_specs.append(slab)
        out_specs.append(slab)
        out_shape.append(jax.ShapeDtypeStruct((rows, cols), BF16))
        args.append(cast_src)
    return pl.pallas_call(
        functools.partial(_inproj_body, with_cast=cast_src is not None),
        grid=(steps,),
        in_specs=in_specs,
        out_specs=out_specs,
        out_shape=out_shape,
        compiler_params=_cparams(("arbitrary",)),
        name="inproj",
    )(*args)


def _group_mean_sq(x, gmat):
    x2 = x * x
    hi = x2.astype(BF16)
    lo = (x2 - hi.astype(F32)).astype(BF16)
    cols = []
    for c in range(x.shape[1] // MXU_DIM):
        sl = slice(c * MXU_DIM, (c + 1) * MXU_DIM)
        cols.append(jnp.dot(hi[:, sl], gmat, preferred_element_type=F32)
                    + jnp.dot(lo[:, sl], gmat, preferred_element_type=F32))
    return cols[0] if len(cols) == 1 else jnp.concatenate(cols, axis=1)


def _tile_lanes(t, reps):
    return t if reps == 1 else jnp.concatenate([t] * reps, axis=1)


def _head_norm_rope(x, gain, cos2, sin2, gmat):
    width = x.shape[1]
    y = x * lax.rsqrt(_group_mean_sq(x, gmat) + EPS) * gain
    lane = lax.broadcasted_iota(jnp.int32, y.shape, 1)
    first_half = (lane & (HEAD_DIM // 2)) == 0
    partner = jnp.where(first_half,
                        pltpu.roll(y, width - HEAD_DIM // 2, axis=1),
                        pltpu.roll(y, HEAD_DIM // 2, axis=1))
    reps = width // LANES
    return y * _tile_lanes(cos2, reps) + partner * _tile_lanes(sin2, reps)


def _rms_rows(x, gain):
    ms = jnp.mean(x * x, axis=-1, keepdims=True)
    return x * lax.rsqrt(ms + EPS) * gain


def _attn_prompt_body(q_ref, k_ref, v_ref, cos_ref, sin_ref, gq_ref, gk_ref, sink_ref, gao_ref, gmat_ref, bias_ref,
                      ao_ref, klast_ref, kprev, vprev, lg_scr, p_scr, *, nb):
    n = pl.program_id(1)
    blk = WINDOW
    rows = GQA * blk

    @pl.when(n == 0)
    def _():
        kprev[...] = jnp.zeros_like(kprev)
        vprev[...] = jnp.zeros_like(vprev)

    gmat = gmat_ref[...]
    cos2 = cos_ref[...]
    sin2 = sin_ref[...]
    q = _head_norm_rope(q_ref[...], gq_ref[...], cos2, sin2, gmat) * (HEAD_DIM ** -0.5)
    k = _head_norm_rope(k_ref[...], gk_ref[...], cos2, sin2, gmat)
    v = v_ref[...]
    kc = jnp.concatenate([kprev[...], k], axis=0).astype(BF16)
    vc = jnp.concatenate([vprev[...], v], axis=0).astype(BF16)
    qb = q.astype(BF16)
    bias = bias_ref[jnp.minimum(n, 1)]

    for g in range(N_KV_HEADS):
        qg = jnp.concatenate(
            [qb[:, (g * GQA + h) * HEAD_DIM:(g * GQA + h + 1) * HEAD_DIM] for h in range(GQA)], axis=0)
        kg = kc[:, g * HEAD_DIM:(g + 1) * HEAD_DIM]
        lg_scr[g * rows:(g + 1) * rows, :] = (
            lax.dot_general(qg, kg, (((1,), (1,)), ((), ())), preferred_element_type=F32) + bias)
    lg = lg_scr[...]
    sink = sink_ref[...]
    m = jnp.maximum(jnp.max(lg, axis=-1, keepdims=True), sink)
    e = jnp.exp(lg - jnp.concatenate([m, m], axis=1))
    den = jnp.sum(e, axis=-1, keepdims=True) + jnp.exp(sink - m)
    p_scr[...] = (e / jnp.concatenate([den, den], axis=1)).astype(BF16)
    pieces = []
    for g in range(N_KV_HEADS):
        vg = vc[:, g * HEAD_DIM:(g + 1) * HEAD_DIM]
        og = jnp.dot(p_scr[g * rows:(g + 1) * rows, :], vg, preferred_element_type=F32)
        pieces.extend(og[h * blk:(h + 1) * blk] for h in range(GQA))
    ao = jnp.concatenate(pieces, axis=1)
    ao_ref[...] = _rms_rows(ao, gao_ref[...]).astype(BF16)

    kprev[...] = k
    vprev[...] = v

    @pl.when(n == nb - 1)
    def _():
        klast_ref[0] = k


def _attn_prompt(q, k, v, cos2, sin2, gq, gk, sink_col, gao, gmat, batch, seq):
    nb = seq // WINDOW
    row = lambda b, n: (b * nb + n, 0)
    const2 = lambda b, n: (0, 0)
    const3 = lambda b, n: (0, 0, 0)
    i = np.arange(GQA * WINDOW)[:, None] % WINDOW
    j = np.arange(2 * WINDOW)[None, :]
    band = (j > i) & (j <= i + WINDOW)
    bias = np.stack([np.where(band & (j >= WINDOW), 0.0, NEG_INF), np.where(band, 0.0, NEG_INF)]).astype(np.float32)
    return pl.pallas_call(
        functools.partial(_attn_prompt_body, nb=nb),
        grid=(batch, nb),
        in_specs=[
            pl.BlockSpec((WINDOW, ATTN_WIDTH), row),
            pl.BlockSpec((WINDOW, KV_WIDTH), row),
            pl.BlockSpec((WINDOW, KV_WIDTH), row),
            pl.BlockSpec((WINDOW, LANES), lambda b, n: (n, 0)),
            pl.BlockSpec((WINDOW, LANES), lambda b, n: (n, 0)),
            pl.BlockSpec((1, ATTN_WIDTH), const2),
            pl.BlockSpec((1, KV_WIDTH), const2),
            pl.BlockSpec((N_Q_HEADS * WINDOW, LANES), const2),
            pl.BlockSpec((1, ATTN_WIDTH), const2),
            pl.BlockSpec((MXU_DIM, MXU_DIM), const2),
            pl.BlockSpec((2, GQA * WINDOW, 2 * WINDOW), const3),
        ],
        out_specs=[
            pl.BlockSpec((WINDOW, ATTN_WIDTH), row),
            pl.BlockSpec((1, WINDOW, KV_WIDTH), lambda b, n: (b, 0, 0)),
        ],
        out_shape=[
            jax.ShapeDtypeStruct((batch * seq, ATTN_WIDTH), BF16),
            jax.ShapeDtypeStruct((batch, WINDOW, KV_WIDTH), F32),
        ],
        scratch_shapes=[
            pltpu.VMEM((WINDOW, KV_WIDTH), F32),
            pltpu.VMEM((WINDOW, KV_WIDTH), F32),
            pltpu.VMEM((N_Q_HEADS * WINDOW, 2 * WINDOW), F32),
            pltpu.VMEM((N_Q_HEADS * WINDOW, 2 * WINDOW), BF16),
        ],
        compiler_params=_cparams(("arbitrary", "arbitrary")),
        name="attn_prompt",
    )(q, k, v, cos2, sin2, gq, gk, jnp.broadcast_to(sink_col.reshape(N_Q_HEADS * WINDOW, 1), (N_Q_HEADS * WINDOW, LANES)),
      gao, gmat, jnp.asarray(bias))


def _attn_sample_body(q_ref, k_ref, v_ref, ck_ref, cv_ref, cos_ref, sin_ref, gq_ref, gk_ref, sink_ref, gao_ref,
                      gmat_ref, ao_ref, ks_ref, vs_ref, *, bb, s):
    gmat = gmat_ref[...]
    cos2 = cos_ref[...]
    sin2 = sin_ref[...]
    q = _head_norm_rope(q_ref[...], gq_ref[...], cos2, sin2, gmat) * (HEAD_DIM ** -0.5)
    k = _head_norm_rope(k_ref[...], gk_ref[...], cos2, sin2, gmat)
    v = v_ref[...]
    ck = ck_ref[...]
    cv = cv_ref[...]
    k3 = k.reshape(bb, s, KV_WIDTH)
    v3 = v.reshape(bb, s, KV_WIDTH)
    ks_ref[:, :WINDOW - s, :] = ck[:, s:, :]
    ks_ref[:, WINDOW - s:, :] = k3
    vs_ref[:, :WINDOW - s, :] = cv[:, s:, :]
    vs_ref[:, WINDOW - s:, :] = v3

    q3 = q.astype(BF16).reshape(bb, s, ATTN_WIDTH)
    ckb = ck.astype(BF16)
    cvb = cv.astype(BF16)
    k3b = k3.astype(BF16)
    v3b = v3.astype(BF16)
    rows = GQA * s
    qi = lax.broadcasted_iota(jnp.int32, (bb, rows, WINDOW), 1) & (s - 1)
    tc = lax.broadcasted_iota(jnp.int32, (bb, rows, WINDOW), 2)
    mask_c = tc > qi
    qi2 = lax.broadcasted_iota(jnp.int32, (bb, rows, s), 1) & (s - 1)
    tn = lax.broadcasted_iota(jnp.int32, (bb, rows, s), 2)
    mask_n = tn <= qi2

    pieces = []
    for g in range(N_KV_HEADS):
        qg = jnp.concatenate(
            [q3[:, :, (g * GQA + h) * HEAD_DIM:(g * GQA + h + 1) * HEAD_DIM] for h in range(GQA)], axis=1)
        sl = slice(g * HEAD_DIM, (g + 1) * HEAD_DIM)
        lc = jnp.einsum("bqd,bkd->bqk", qg, ckb[:, :, sl], preferred_element_type=F32)
        ln = jnp.einsum("bqd,bkd->bqk", qg, k3b[:, :, sl], preferred_element_type=F32)
        lc = jnp.where(mask_c, lc, NEG_INF)
        ln = jnp.where(mask_n, ln, NEG_INF)
        sink = sink_ref[g]
        m = jnp.maximum(jnp.maximum(jnp.max(lc, axis=-1, keepdims=True), jnp.max(ln, axis=-1, keepdims=True)), sink)
        ec = jnp.exp(lc - m)
        en = jnp.exp(ln - m[:, :, :s])
        den = jnp.sum(ec, axis=-1, keepdims=True) + jnp.sum(en, axis=-1, keepdims=True) + jnp.exp(sink - m)
        og = (jnp.einsum("bqk,bkd->bqd", (ec / den).astype(BF16), cvb[:, :, sl], preferred_element_type=F32)
              + jnp.einsum("bqk,bkd->bqd", (en / den[:, :, :s]).astype(BF16), v3b[:, :, sl], preferred_element_type=F32))
        pieces.extend(og[:, h * s:(h + 1) * s, :] for h in range(GQA))
    ao = jnp.concatenate(pieces, axis=2).reshape(bb * s, ATTN_WIDTH)
    ao_ref[...] = _rms_rows(ao, gao_ref[...]).astype(BF16)


def _attn_sample(q, k, v, ck, cv, cos2, sin2, gq, gk, sink_col, gao, gmat, batch, s, bb):
    rows = bb * s
    row = lambda i: (i, 0)
    const2 = lambda i: (0, 0)
    b3 = lambda i: (i, 0, 0)
    return pl.pallas_call(
        functools.partial(_attn_sample_body, bb=bb, s=s),
        grid=(batch // bb,),
        in_specs=[
            pl.BlockSpec((rows, ATTN_WIDTH), row),
            pl.BlockSpec((rows, KV_WIDTH), row),
            pl.BlockSpec((rows, KV_WIDTH), row),
            pl.BlockSpec((bb, WINDOW, KV_WIDTH), b3),
            pl.BlockSpec((bb, WINDOW, KV_WIDTH), b3),
            pl.BlockSpec((rows, LANES), const2),
            pl.BlockSpec((rows, LANES), const2),
            pl.BlockSpec((1, ATTN_WIDTH), const2),
            pl.BlockSpec((1, KV_WIDTH), const2),
            pl.BlockSpec((N_KV_HEADS, GQA * s, LANES), lambda i: (0, 0, 0)),
            pl.BlockSpec((1, ATTN_WIDTH), const2),
            pl.BlockSpec((MXU_DIM, MXU_DIM), const2),
        ],
        out_specs=[
            pl.BlockSpec((rows, ATTN_WIDTH), row),
            pl.BlockSpec((bb, WINDOW, KV_WIDTH), b3),
            pl.BlockSpec((bb, WINDOW, KV_WIDTH), b3),
        ],
        out_shape=[
            jax.ShapeDtypeStruct((batch * s, ATTN_WIDTH), BF16),
            jax.ShapeDtypeStruct((batch, WINDOW, KV_WIDTH), F32),
            jax.ShapeDtypeStruct((batch, WINDOW, KV_WIDTH), F32),
        ],
        compiler_params=_cparams(("arbitrary",)),
        name="attn_sample",
    )(q, k, v, ck, cv, cos2, sin2, gq, gk, jnp.broadcast_to(sink_col, (N_KV_HEADS, GQA * s, LANES)), gao, gmat)


def _lru_gates(xc, wa_ref, wx_ref, ba, bx, lam):
    xb = xc.astype(BF16)
    r_cols, i_cols = [], []
    for c in range(LRU_WIDTH // MXU_DIM):
        sl = slice(c * MXU_DIM, (c + 1) * MXU_DIM)
        r_cols.append(jnp.dot(xb[:, sl], wa_ref[c], preferred_element_type=F32))
        i_cols.append(jnp.dot(xb[:, sl], wx_ref[c], preferred_element_type=F32))
    r = 1.0 / (1.0 + jnp.exp(-(jnp.concatenate(r_cols, axis=1) + ba)))
    ig = 1.0 / (1.0 + jnp.exp(-(jnp.concatenate(i_cols, axis=1) + bx)))
    neg_lam = -lam
    softplus = jnp.maximum(neg_lam, 0.0) + jnp.log1p(jnp.exp(-jnp.abs(neg_lam)))
    log_a = -LRU_C * r * softplus
    a = jnp.exp(log_a)
    one_minus_a2 = -jnp.tanh(log_a) * (a * a + 1.0)
    root = jnp.where(one_minus_a2 > 0.0, one_minus_a2 * lax.rsqrt(one_minus_a2), 0.0)
    u = root * (ig * xc)
    return a, u


def _segment_scan(a, u):
    rows, width = a.shape
    a = a.reshape(rows // SUBLANES, SUBLANES, width)
    u = u.reshape(rows // SUBLANES, SUBLANES, width)
    t = lax.broadcasted_iota(jnp.int32, a.shape, 1)
    d = 1
    while d < SUBLANES:
        valid = t >= d
        u = jnp.where(valid, a * pltpu.roll(u, d, axis=1) + u, u)
        a = jnp.where(valid, a * pltpu.roll(a, d, axis=1), a)
        d *= 2
    return a.reshape(rows, width), u.reshape(rows, width)


def _lru_prompt_body(xr_ref, yg_ref, cw_ref, cb_ref, wa_ref, wx_ref, ba_ref, bx_ref, lam_ref, glo_ref,
                     *refs, tb, with_transpose):
    if with_transpose:
        src_ref, lo_ref, hlast_ref, dst_ref, xpad, hcarry = refs
        dst_ref[...] = src_ref[...].T.astype(BF16)
    else:
        lo_ref, hlast_ref, xpad, hcarry = refs
    t_blk = pl.program_id(1)

    @pl.when(t_blk == 0)
    def _():
        xpad[:SUBLANES, :] = jnp.zeros((SUBLANES, LRU_WIDTH), F32)
        hcarry[...] = jnp.zeros_like(hcarry)

    xr = xr_ref[...]
    xpad[SUBLANES:, :] = xr
    xc = xr * cw_ref[CONV_W - 1:CONV_W, :] + cb_ref[...]
    for kshift in range(1, CONV_W):
        shifted = xpad[SUBLANES - kshift:SUBLANES - kshift + tb, :]
        xc = xc + shifted * cw_ref[CONV_W - 1 - kshift:CONV_W - kshift, :]
    a, u = _lru_gates(xc, wa_ref, wx_ref, ba_ref[...], bx_ref[...], lam_ref[...])
    acum, hloc = _segment_scan(a, u)
    tiles = []
    carry = hcarry[...]
    for r in range(tb // SUBLANES):
        rows = slice(r * SUBLANES, (r + 1) * SUBLANES)
        h_tile = hloc[rows] + acum[rows] * carry
        tiles.append(h_tile)
        carry = h_tile[SUBLANES - 1:SUBLANES]
    h = jnp.concatenate(tiles, axis=0)
    y = h * jax.nn.gelu(yg_ref[...])
    lo_ref[...] = _rms_rows(y, glo_ref[...]).astype(BF16)
    hlast = h[tb - 1:tb, :]
    hcarry[...] = hlast
    xpad[:SUBLANES, :] = xr[tb - SUBLANES:, :]
    hlast_ref[0] = hlast


def _lru_prompt(xr, yg, cw, cb, wa_bd, wx_bd, ba, bx, lam, glo, batch, seq, tb, transpose_src=None):
    nt = seq // tb
    row = lambda b, t: (b * nt + t, 0)
    const2 = lambda b, t: (0, 0)
    const3 = lambda b, t: (0, 0, 0)
    nchunk = LRU_WIDTH // MXU_DIM
    in_specs = [
        pl.BlockSpec((tb, LRU_WIDTH), row),
        pl.BlockSpec((tb, LRU_WIDTH), row),
        pl.BlockSpec((CONV_W, LRU_WIDTH), const2),
        pl.BlockSpec((1, LRU_WIDTH), const2),
        pl.BlockSpec((nchunk, MXU_DIM, MXU_DIM), const3),
        pl.BlockSpec((nchunk, MXU_DIM, MXU_DIM), const3),
        pl.BlockSpec((1, LRU_WIDTH), const2),
        pl.BlockSpec((1, LRU_WIDTH), const2),
        pl.BlockSpec((1, LRU_WIDTH), const2),
        pl.BlockSpec((1, LRU_WIDTH), const2),
    ]
    out_specs = [
        pl.BlockSpec((tb, LRU_WIDTH), row),
        pl.BlockSpec((1, 1, LRU_WIDTH), lambda b, t: (b, 0, 0)),
    ]
    out_shape = [
        jax.ShapeDtypeStruct((batch * seq, LRU_WIDTH), BF16),
        jax.ShapeDtypeStruct((batch, 1, LRU_WIDTH), F32),
    ]
    args = [xr, yg, cw, cb, wa_bd, wx_bd, ba, bx, lam, glo]
    if transpose_src is not None:
        rows, cols = transpose_src.shape
        slab = rows // (batch * nt)
        in_specs.append(pl.BlockSpec((slab, cols), row))
        out_specs.append(pl.BlockSpec((cols, slab), lambda b, t: (0, b * nt + t)))
        out_shape.append(jax.ShapeDtypeStruct((cols, rows), BF16))
        args.append(transpose_src)
    return pl.pallas_call(
        functools.partial(_lru_prompt_body, tb=tb, with_transpose=transpose_src is not None),
        grid=(batch, nt),
        in_specs=in_specs,
        out_specs=out_specs,
        out_shape=out_shape,
        scratch_shapes=[pltpu.VMEM((SUBLANES + tb, LRU_WIDTH), F32), pltpu.VMEM((1, LRU_WIDTH), F32)],
        compiler_params=_cparams(("arbitrary", "arbitrary")),
        name="lru_prompt",
    )(*args)


def _lru_sample_body(xr_ref, yg_ref, cpad_ref, h0_ref, cw_ref, cb_ref, wa_ref, wx_ref, ba_ref, bx_ref, lam_ref,
                     glo_ref, lo_ref, h_ref, *, s):
    xr = xr_ref[...]
    cpad = cpad_ref[...]
    rows = xr.shape[0]
    t = lax.broadcasted_iota(jnp.int32, xr.shape, 0) & (s - 1)
    xc = xr * cw_ref[CONV_W - 1:CONV_W, :] + cb_ref[...]
    for kshift in range(1, CONV_W):
        shifted = jnp.where(t < kshift, pltpu.roll(cpad, rows - s + kshift, axis=0), pltpu.roll(xr, kshift, axis=0))
        xc = xc + shifted * cw_ref[CONV_W - 1 - kshift:CONV_W - kshift, :]
    a, u = _lru_gates(xc, wa_ref, wx_ref, ba_ref[...], bx_ref[...], lam_ref[...])
    assert s == SUBLANES
    acum, hloc = _segment_scan(a, u)
    h = hloc + acum * h0_ref[...]
    h_ref[...] = h
    y = h * jax.nn.gelu(yg_ref[...])
    lo_ref[...] = _rms_rows(y, glo_ref[...]).astype(BF16)


def _lru_sample(xr, yg, cpad, h0rows, cw, cb, wa_bd, wx_bd, ba, bx, lam, glo, s, tb):
    n = xr.shape[0]
    row = lambda i: (i, 0)
    const2 = lambda i: (0, 0)
    const3 = lambda i: (0, 0, 0)
    nchunk = LRU_WIDTH // MXU_DIM
    return pl.pallas_call(
        functools.partial(_lru_sample_body, s=s),
        grid=(n // tb,),
        in_specs=[
            pl.BlockSpec((tb, LRU_WIDTH), row),
            pl.BlockSpec((tb, LRU_WIDTH), row),
            pl.BlockSpec((tb, LRU_WIDTH), row),
            pl.BlockSpec((tb, LRU_WIDTH), row),
            pl.BlockSpec((CONV_W, LRU_WIDTH), const2),
            pl.BlockSpec((1, LRU_WIDTH), const2),
            pl.BlockSpec((nchunk, MXU_DIM, MXU_DIM), const3),
            pl.BlockSpec((nchunk, MXU_DIM, MXU_DIM), const3),
            pl.BlockSpec((1, LRU_WIDTH), const2),
            pl.BlockSpec((1, LRU_WIDTH), const2),
            pl.BlockSpec((1, LRU_WIDTH), const2),
            pl.BlockSpec((1, LRU_WIDTH), const2),
        ],
        out_specs=[pl.BlockSpec((tb, LRU_WIDTH), row), pl.BlockSpec((tb, LRU_WIDTH), row)],
        out_shape=[jax.ShapeDtypeStruct((n, LRU_WIDTH), BF16), jax.ShapeDtypeStruct((n, LRU_WIDTH), F32)],
        compiler_params=_cparams(("arbitrary",)),
        name="lru_sample",
    )(xr, yg, cpad, h0rows, cw, cb, wa_bd, wx_bd, ba, bx, lam, glo)


def _outproj_body(x_ref, ao_ref, lo_ref, w_ref, g_ref, x1_ref, hnt_ref):
    mixed = (jnp.dot(ao_ref[...], w_ref[:ATTN_WIDTH, :], preferred_element_type=F32)
             + jnp.dot(lo_ref[...], w_ref[ATTN_WIDTH:, :], preferred_element_type=F32))
    x1 = x_ref[...] + mixed
    x1_ref[...] = x1
    hnt_ref[...] = _rms_rows(x1, g_ref[...]).T.astype(BF16)


def _outproj(x, ao, lo, w_bf, g, tm):
    n = x.shape[0]
    row = lambda i: (i, 0)
    const2 = lambda i: (0, 0)
    return pl.pallas_call(
        _outproj_body,
        grid=(n // tm,),
        in_specs=[
            pl.BlockSpec((tm, D_MODEL), row),
            pl.BlockSpec((tm, ATTN_WIDTH), row),
            pl.BlockSpec((tm, LRU_WIDTH), row),
            pl.BlockSpec((D_MODEL, D_MODEL), const2),
            pl.BlockSpec((1, D_MODEL), const2),
        ],
        out_specs=[pl.BlockSpec((tm, D_MODEL), row), pl.BlockSpec((D_MODEL, tm), lambda i: (0, i))],
        out_shape=[jax.ShapeDtypeStruct((n, D_MODEL), F32), jax.ShapeDtypeStruct((D_MODEL, n), BF16)],
        compiler_params=_cparams(("arbitrary",)),
        name="outproj",
    )(x, ao, lo, w_bf, g)


PEER_CAND_COLS = tuple(min(PEER_TOPK, (PEER_TOPK + 1) // (a + 1)) for a in range(PEER_TOPK))
ROUTE_LANES = 128
ROUTE_TILES_PER_STEP = 4


def _sort_network(n):
    pairs, p = [], 1
    while p < n:
        k = p
        while k >= 1:
            for j in range(k % p, n - k, 2 * k):
                for i in range(min(k, n - j - k)):
                    if (i + j) // (2 * p) == (i + j + k) // (2 * p):
                        pairs.append((i + j, i + j + k))
            k //= 2
        p *= 2
    return tuple(pairs)


SORT16 = _sort_network(N_KEYS // SUBLANES)


def _pop_lists(heads_and_ids, nvals):
    levels = [list(lv) for lv, _ in heads_and_ids]
    ids = [i for _, i in heads_and_ids]
    big = float(SUBLANES * len(levels))
    vals = []
    for r in range(nvals):
        top = levels[0][0]
        for lv in levels[1:]:
            top = jnp.maximum(top, lv[0])
        m = jnp.max(top, axis=0, keepdims=True)
        vals.append(m)
        if r == nvals - 1:
            break
        first = None
        for lv, i in zip(levels, ids):
            cand = jnp.where(lv[0] == m, i, big)
            first = cand if first is None else jnp.minimum(first, cand)
        first = jnp.min(first, axis=0, keepdims=True)
        for lv, i in zip(levels, ids):
            hit = i == first
            live = min(len(lv), nvals - 1 - r)
            for d in range(live):
                below = lv[d + 1] if d + 1 < len(lv) else -jnp.inf
                lv[d] = jnp.where(hit, below, lv[d])
    return vals


def _top16(s, ltri, ones8):
    lanes = s.shape[1]
    lv = [s[SUBLANES * r:SUBLANES * (r + 1)] for r in range(N_KEYS // SUBLANES)]
    for i, j in SORT16:
        lv[i], lv[j] = jnp.maximum(lv[i], lv[j]), jnp.minimum(lv[i], lv[j])
    subf = lax.broadcasted_iota(jnp.int32, (SUBLANES, lanes), 0).astype(F32)
    vals = _pop_lists([(lv, subf)], PEER_TOPK)
    v16 = vals[PEER_TOPK - 1]
    gt = s > v16
    eq = s == v16
    n_gt = jnp.dot(ones8, jnp.where(gt, 1.0, 0.0).astype(BF16), preferred_element_type=F32)[0:1]
    eq_before = jnp.dot(ltri, jnp.where(eq, 1.0, 0.0).astype(BF16), preferred_element_type=F32)
    need = float(PEER_TOPK) - n_gt
    kept = jnp.where(gt, s, jnp.where(eq, jnp.where(eq_before < need, s, -jnp.inf), -jnp.inf))
    return vals, kept


def _best_sums(v0, v1):
    lanes = v0[0].shape[1]
    lo0 = jnp.concatenate(v0[:SUBLANES], axis=0)
    hi0 = jnp.concatenate(v0[SUBLANES:], axis=0)
    row8 = lax.broadcasted_iota(jnp.int32, (SUBLANES, lanes), 0)
    subf = row8.astype(F32)
    levels = []
    for b in range(PEER_TOPK):
        c = lo0 + v1[b]
        nrows = sum(1 for a in range(SUBLANES) if PEER_CAND_COLS[a] > b)
        levels.append(c if nrows == SUBLANES else jnp.where(row8 < nrows, c, -jnp.inf))
    return _pop_lists([(levels, subf), ([hi0 + v1[0]], subf + float(SUBLANES))], PEER_TOPK + 1)


def _route_body(hnt_ref, wq_ref, keys_ref, ltri_ref, ones_ref, s1m_ref, e1z_ref, t0_ref, e0_ref, s_scr, *, tm):
    qt = jnp.dot(wq_ref[...], hnt_ref[...], preferred_element_type=F32)
    half = D_KEY // 2
    for hc in range(2 * PEER_HEADS):
        qhc = qt[hc * half:(hc + 1) * half, :].astype(BF16)
        s_scr[hc] = jnp.dot(keys_ref[hc], qhc, preferred_element_type=F32)

    def route_tile(h, lanes):
        ltri = ltri_ref[...]
        ones8 = ones_ref[...]
        v0, s0m = _top16(s_scr[2 * h, :, lanes], ltri, ones8)
        v1, s1m = _top16(s_scr[2 * h + 1, :, lanes], ltri, ones8)
        best = _best_sums(v0, v1)
        mx = best[0]
        z = jnp.exp(best[0] - mx)
        for r in range(1, PEER_TOPK):
            z = z + jnp.exp(best[r] - mx)
        thr = 0.5 * (best[PEER_TOPK - 1] + best[PEER_TOPK])
        s1m_ref[h, :, lanes] = s1m
        e1z_ref[h, :, lanes] = jnp.exp(s1m - v1[0]) / z
        tiled = (N_KEYS // SUBLANES, SUBLANES, ROUTE_LANES)
        t0_ref[h, :, :, lanes] = (thr - s0m).reshape(tiled)
        e0_ref[h, :, :, lanes] = jnp.exp(s0m - v0[0]).reshape(tiled)

    per_head = tm // (ROUTE_TILES_PER_STEP * ROUTE_LANES)

    def step(it, carry):
        h = it // per_head
        c = it % per_head
        for t in range(ROUTE_TILES_PER_STEP):
            start = (c * ROUTE_TILES_PER_STEP + t) * ROUTE_LANES
            route_tile(h, pl.ds(pl.multiple_of(start, ROUTE_LANES), ROUTE_LANES))
        return carry

    lax.fori_loop(0, PEER_HEADS * per_head, step, 0)


def _route(hnt, wq_t, keys, tm):
    n = hnt.shape[1]
    blk = lambda i: (0, 0, i)
    shape = jax.ShapeDtypeStruct((PEER_HEADS, N_KEYS, n), F32)
    spec = pl.BlockSpec((PEER_HEADS, N_KEYS, tm), blk)
    shape4 = jax.ShapeDtypeStruct((PEER_HEADS, N_KEYS // SUBLANES, SUBLANES, n), F32)
    spec4 = pl.BlockSpec((PEER_HEADS, N_KEYS // SUBLANES, SUBLANES, tm), lambda i: (0, 0, 0, i))
    ltri = jnp.asarray(np.tril(np.ones((N_KEYS, N_KEYS)), -1), BF16)
    ones8 = jnp.ones((SUBLANES, N_KEYS), BF16)
    return pl.pallas_call(
        functools.partial(_route_body, tm=tm),
        grid=(n // tm,),
        in_specs=[
            pl.BlockSpec((D_MODEL, tm), lambda i: (0, i)),
            pl.BlockSpec((PEER_HEADS * D_KEY, D_MODEL), lambda i: (0, 0)),
            pl.BlockSpec((2 * PEER_HEADS, N_KEYS, D_KEY // 2), lambda i: (0, 0, 0)),
            pl.BlockSpec((N_KEYS, N_KEYS), lambda i: (0, 0)),
            pl.BlockSpec((SUBLANES, N_KEYS), lambda i: (0, 0)),
        ],
        out_specs=[spec, spec, spec4, spec4],
        out_shape=[shape, shape, shape4, shape4],
        scratch_shapes=[pltpu.VMEM((2 * PEER_HEADS, N_KEYS, tm), F32)],
        compiler_params=_cparams(("arbitrary",)),
        name="peer_route",
    )(hnt, wq_t, keys, ltri, ones8)


EXPERT_LANES = 256
EXPERT_BLOCK = SUBLANES * N_KEYS
EXPERT_ROWS = EXPERT_BLOCK // N_KEYS
U_KCHUNK = 512
U_NCHUNK = D_MODEL // U_KCHUNK
V_KCHUNK = 512
V_NCHUNK = EXPERT_BLOCK // V_KCHUNK


def _routing_weight(s1m_ref, e1z_ref, t0_ref, e0_ref, k, ii, lanes):
    width = lanes.stop - lanes.start
    trow = [jnp.broadcast_to(t0_ref[h, k, ii:ii + 1, lanes], (SUBLANES, width)) for h in range(PEER_HEADS)]
    e0row = [jnp.broadcast_to(e0_ref[h, k, ii:ii + 1, lanes], (SUBLANES, width)) for h in range(PEER_HEADS)]
    out = []
    for j in range(0, N_KEYS, SUBLANES):
        w = None
        for h in range(PEER_HEADS):
            term = jnp.where(s1m_ref[h, j:j + SUBLANES, lanes] >= trow[h], e1z_ref[h, j:j + SUBLANES, lanes], 0.0) * e0row[h]
            w = term if w is None else w + term
        out.append(w)
    return jnp.concatenate(out, axis=0)


def _experts_body(hnt_ref, *refs, tm, nk):
    u_refs = refs[:U_NCHUNK]
    vt_refs = refs[U_NCHUNK:U_NCHUNK + V_NCHUNK]
    (s1m_ref, e1z_ref, t0_ref, e0_ref, x1_ref, y_ref, acc, act, wbuf, pt_a, pt_b) = refs[U_NCHUNK + V_NCHUNK:]
    k = pl.program_id(1)
    kt = jnp.minimum(k, nk - 1)
    nlane = tm // EXPERT_LANES

    def weights():
        for ii in range(EXPERT_ROWS):
            for c in range(nlane):
                rows = slice(ii * N_KEYS, (ii + 1) * N_KEYS)
                lanes = slice(c * EXPERT_LANES, (c + 1) * EXPERT_LANES)
                wbuf[rows, lanes] = _routing_weight(s1m_ref, e1z_ref, t0_ref, e0_ref, kt, ii, lanes)

    def dot1():
        for c in range(nlane):
            lanes = slice(c * EXPERT_LANES, (c + 1) * EXPERT_LANES)
            a = None
            for kc in range(U_NCHUNK):
                d = jnp.dot(u_refs[kc][...], hnt_ref[kc * U_KCHUNK:(kc + 1) * U_KCHUNK, lanes],
                            preferred_element_type=F32)
                a = d if a is None else a + d
            act[:, lanes] = a

    def weighted_act(pt):
        for ii in range(EXPERT_ROWS):
            rows = slice(ii * N_KEYS, (ii + 1) * N_KEYS)
            for c in range(nlane):
                lanes = slice(c * EXPERT_LANES, (c + 1) * EXPERT_LANES)
                pt[rows, lanes] = (wbuf[rows, lanes] * jax.nn.gelu(act[rows, lanes])).astype(BF16)

    def dot2(pt):
        a = None
        for c in range(V_NCHUNK):
            d = jnp.dot(vt_refs[c][...], pt[c * V_KCHUNK:(c + 1) * V_KCHUNK, :], preferred_element_type=F32)
            a = d if a is None else a + d
        acc[...] += a

    @pl.when(k == 0)
    def _():
        acc[...] = jnp.zeros_like(acc)
        pt_b[...] = jnp.zeros_like(pt_b)

    @pl.when(k < nk)
    def _():
        weights()
        dot1()

    def phase2(pt_new, pt_old):
        weighted_act(pt_new)
        dot2(pt_old)

    @pl.when(k % 2 == 0)
    def _():
        phase2(pt_a, pt_b)

    @pl.when(k % 2 == 1)
    def _():
        phase2(pt_b, pt_a)

    @pl.when(k == nk)
    def _():
        y_ref[...] = x1_ref[...] + acc[...].T


def _experts(hnt, u_bf, vt_bf, s1m, e1z, t0, e0, x1, tm):
    n = hnt.shape[1]
    nk = N_EXPERTS // EXPERT_BLOCK
    assert nk % 2 == 0
    tok = lambda t, k: (t, 0)
    once = pl.Buffered(1)
    small = pl.BlockSpec((PEER_HEADS, N_KEYS, tm), lambda t, k: (0, 0, t), pipeline_mode=once)
    tiled = pl.BlockSpec((PEER_HEADS, N_KEYS // SUBLANES, SUBLANES, tm), lambda t, k: (0, 0, 0, t), pipeline_mode=once)
    u_specs = [pl.BlockSpec((EXPERT_BLOCK, U_KCHUNK), lambda t, k, kc=kc: (jnp.minimum(k, nk - 1), kc))
               for kc in range(U_NCHUNK)]
    vt_specs = [pl.BlockSpec((D_MODEL, V_KCHUNK), lambda t, k, c=c: (0, V_NCHUNK * jnp.maximum(k - 1, 0) + c))
                for c in range(V_NCHUNK)]
    return pl.pallas_call(
        functools.partial(_experts_body, tm=tm, nk=nk),
        grid=(n // tm, nk + 1),
        in_specs=[
            pl.BlockSpec((D_MODEL, tm), lambda t, k: (0, t), pipeline_mode=once),
            *u_specs,
            *vt_specs,
            small, small, tiled, tiled,
            pl.BlockSpec((tm, D_MODEL), tok),
        ],
        out_specs=pl.BlockSpec((tm, D_MODEL), tok),
        out_shape=jax.ShapeDtypeStruct((n, D_MODEL), F32),
        scratch_shapes=[
            pltpu.VMEM((D_MODEL, tm), F32),
            pltpu.VMEM((EXPERT_BLOCK, tm), F32),
            pltpu.VMEM((EXPERT_BLOCK, tm), F32),
            pltpu.VMEM((EXPERT_BLOCK, tm), BF16),
            pltpu.VMEM((EXPERT_BLOCK, tm), BF16),
        ],
        compiler_params=_cparams(("arbitrary", "arbitrary")),
        name="peer_experts",
    )(hnt, *([u_bf] * len(u_specs)), *([vt_bf] * len(vt_specs)), s1m, e1z, t0, e0, x1)


def _rope_tables(pos):
    half = HEAD_DIM // 2
    inv = jnp.exp(-math.log(ROPE_THETA) * jnp.arange(half, dtype=F32) * (2.0 / HEAD_DIM))
    ang = pos[:, None] * inv[None, :]
    cos, sin = jnp.cos(ang), jnp.sin(ang)
    cos2 = jnp.concatenate([cos, cos, cos, cos], axis=1)
    sin2 = jnp.concatenate([-sin, sin, -sin, sin], axis=1)
    return cos2, sin2


def _block_diag(w):
    per = MXU_DIM // LRU_BLOCK
    w4 = w.reshape(LRU_WIDTH // MXU_DIM, per, LRU_BLOCK, LRU_BLOCK)
    eye = jnp.eye(per, dtype=w.dtype)
    return jnp.einsum("cpde,pq->cpdqe", w4, eye).reshape(LRU_WIDTH // MXU_DIM, MXU_DIM, MXU_DIM).astype(BF16)


def kernel(x_prompt, x_sample, cache_k, cache_v, state_conv, state_h, norm_mix_g, w_in, q_norm_g, k_norm_g, attn_sinks, conv_w, conv_b, w_rec_gate, b_rec_gate, w_in_gate, b_in_gate, lru_lambda, attn_out_g, lru_out_g, w_out, norm_ffn_g, w_peer_q, peer_sub_keys, peer_u, peer_v):
    depth = w_in.shape[0]
    assert depth == 1
    batch, seq, _ = x_prompt.shape
    dbatch, dseq, _ = x_sample.shape
    past_len = 8192
    l = 0

    w_in_bf = w_in[l].astype(BF16)
    w_out_bf = w_out[l].astype(BF16)
    wq_t = w_peer_q[l].T.astype(BF16)
    keys = peer_sub_keys[l].reshape(2 * PEER_HEADS, N_KEYS, D_KEY // 2).astype(BF16)
    wa_bd = _block_diag(w_rec_gate[l])
    wx_bd = _block_diag(w_in_gate[l])
    ba = b_rec_gate[l].reshape(1, LRU_WIDTH)
    bx = b_in_gate[l].reshape(1, LRU_WIDTH)
    lam = lru_lambda[l].reshape(1, LRU_WIDTH)
    cw = conv_w[l]
    cb = conv_b[l].reshape(1, LRU_WIDTH)
    g_mix = norm_mix_g[l].reshape(1, D_MODEL)
    g_ffn = norm_ffn_g[l].reshape(1, D_MODEL)
    gq = jnp.tile(q_norm_g[l], N_Q_HEADS).reshape(1, ATTN_WIDTH)
    gk = jnp.tile(k_norm_g[l], N_KV_HEADS).reshape(1, KV_WIDTH)
    gao = attn_out_g[l].reshape(1, ATTN_WIDTH)
    glo = lru_out_g[l].reshape(1, LRU_WIDTH)
    gmat = jnp.asarray(np.kron(np.eye(MXU_DIM // HEAD_DIM), np.full((HEAD_DIM, HEAD_DIM), 1.0 / HEAD_DIM)), BF16)
    sinks = attn_sinks[l].astype(F32).reshape(N_KV_HEADS, GQA)
    sink_p = jnp.repeat(sinks, WINDOW, axis=1).reshape(N_KV_HEADS, GQA * WINDOW, 1)
    sink_s = jnp.repeat(sinks, dseq, axis=1).reshape(N_KV_HEADS, GQA * dseq, 1)
    cos_p, sin_p = _rope_tables(jnp.arange(seq, dtype=F32))
    cos_s, sin_s = _rope_tables(float(past_len) + jnp.arange(dseq, dtype=F32))

    n_p = batch * seq
    n_s = dbatch * dseq
    xp2 = x_prompt.reshape(n_p, D_MODEL)
    xs2 = x_sample.reshape(n_s, D_MODEL)

    qp, kp, vp, xrp, ygp, u_bf = _inproj(xp2, g_mix, w_in_bf, 256, cast_src=peer_u[l])
    ao_p, k_last = _attn_prompt(qp, kp, vp, cos_p, sin_p, gq, gk, sink_p, gao, gmat, batch, seq)
    lo_p, h_p, vt_bf = _lru_prompt(xrp, ygp, cw, cb, wa_bd, wx_bd, ba, bx, lam, glo, batch, seq, 256,
                                   transpose_src=peer_v[l])
    x1p, hn_p = _outproj(xp2, ao_p, lo_p, w_out_bf, g_ffn, 256)

    bb = 8
    qs, ks, vs, xrs, ygs = _inproj(xs2, g_mix, w_in_bf, 256)
    cos_sb = jnp.tile(cos_s, (bb, 1))
    sin_sb = jnp.tile(sin_s, (bb, 1))
    ck = cache_k[l].reshape(dbatch, WINDOW, KV_WIDTH)
    cv = cache_v[l].reshape(dbatch, WINDOW, KV_WIDTH)
    ao_s, k_s, v_s = _attn_sample(qs, ks, vs, ck, cv, cos_sb, sin_sb, gq, gk, sink_s, gao, gmat, dbatch, dseq, bb)
    cpad = jnp.pad(state_conv[l], ((0, 0), (dseq - (CONV_W - 1), 0), (0, 0))).reshape(n_s, LRU_WIDTH)
    h0rows = jnp.repeat(state_h[l], dseq, axis=0)
    lo_s, h_s = _lru_sample(xrs, ygs, cpad, h0rows, cw, cb, wa_bd, wx_bd, ba, bx, lam, glo, dseq, 256)
    x1s, hn_s = _outproj(xs2, ao_s, lo_s, w_out_bf, g_ffn, 256)

    tm = 512
    outs = []
    for x1, hn in ((x1p, hn_p), (x1s, hn_s)):
        s1m, e1z, t0, e0 = _route(hn, wq_t, keys, tm)
        outs.append(_experts(hn, u_bf, vt_bf, s1m, e1z, t0, e0, x1, tm))
    y_p = outs[0].reshape(batch, seq, D_MODEL)
    y_s = outs[1].reshape(dbatch, dseq, D_MODEL)

    kv_shape = (N_KV_HEADS, HEAD_DIM)
    k_prompt = k_last.reshape(1, batch, WINDOW, *kv_shape)
    v_prompt = vp.reshape(batch, seq, *kv_shape)[:, -WINDOW:][None]
    conv_prompt = xrp.reshape(batch, seq, LRU_WIDTH)[:, -(CONV_W - 1):][None]
    h_prompt = h_p.reshape(1, batch, LRU_WIDTH)
    k_sample = k_s.reshape(1, dbatch, WINDOW, *kv_shape)
    v_sample = v_s.reshape(1, dbatch, WINDOW, *kv_shape)
    conv_sample = xrs.reshape(dbatch, dseq, LRU_WIDTH)[:, -(CONV_W - 1):][None]
    h_sample = h_s.reshape(dbatch, dseq, LRU_WIDTH)[:, -1][None]
    return (y_p, y_s, k_prompt, v_prompt, conv_prompt, h_prompt, k_sample, v_sample, conv_sample, h_sample)
```

```python
import functools
import math

import jax
import jax.numpy as jnp
import numpy as np
from jax import lax
from jax.experimental import pallas as pl
from jax.experimental.pallas import tpu as pltpu

F32 = jnp.float32
BF16 = jnp.bfloat16

D_MODEL = 2048
HEAD_DIM = 64
N_Q_HEADS = 16
N_KV_HEADS = 4
GQA = N_Q_HEADS // N_KV_HEADS
ATTN_WIDTH = N_Q_HEADS * HEAD_DIM
KV_WIDTH = N_KV_HEADS * HEAD_DIM
WINDOW = 128
ROPE_THETA = 10000.0
LRU_WIDTH = D_MODEL - ATTN_WIDTH
LRU_BLOCK = 64
CONV_W = 4
LRU_C = 8.0
N_KEYS = 128
N_EXPERTS = N_KEYS * N_KEYS
PEER_HEADS = 8
PEER_TOPK = 16
PAST_LEN = 8192
D_KEY = 256
EPS = 1e-6
NEG_INF = -1e30

MXU_DIM = 256
LANES = 128
SUBLANES = 8
VMEM_LIMIT_BYTES = 60 * 1024 * 1024

TOKENS_INPROJ = 256
TOKENS_OUTPROJ = 512
ROWS_LRU = 256
SEQS_ATTN_SAMPLE = 8
TOKENS_PEER = 512


def _cparams(sem):
    return pltpu.CompilerParams(dimension_semantics=sem, vmem_limit_bytes=VMEM_LIMIT_BYTES)


IN_CUTS = (0, ATTN_WIDTH, ATTN_WIDTH + KV_WIDTH, ATTN_WIDTH + 2 * KV_WIDTH,
           ATTN_WIDTH + 2 * KV_WIDTH + LRU_WIDTH, ATTN_WIDTH + 2 * KV_WIDTH + 2 * LRU_WIDTH)


def _inproj_body(x_ref, g_ref, w_ref, *refs, with_cast):
    if with_cast:
        src_ref, q_ref, k_ref, v_ref, xr_ref, yg_ref, dst_ref = refs
        dst_ref[...] = src_ref[...].astype(BF16)
    else:
        q_ref, k_ref, v_ref, xr_ref, yg_ref = refs
    x = x_ref[...]
    ms = jnp.mean(x * x, axis=-1, keepdims=True)
    hn = (x * lax.rsqrt(ms + EPS) * g_ref[...]).astype(BF16)
    for o_ref, lo, hi in zip((q_ref, k_ref, v_ref, xr_ref, yg_ref), IN_CUTS[:-1], IN_CUTS[1:]):
        o_ref[...] = jnp.dot(hn, w_ref[:, lo:hi], preferred_element_type=F32)


def _inproj(x, g, w_bf, tm, cast_src=None):
    n = x.shape[0]
    steps = n // tm
    widths = [hi - lo for lo, hi in zip(IN_CUTS[:-1], IN_CUTS[1:])]
    in_specs = [
        pl.BlockSpec((tm, D_MODEL), lambda i: (i, 0)),
        pl.BlockSpec((1, D_MODEL), lambda i: (0, 0)),
        pl.BlockSpec((D_MODEL, IN_CUTS[-1]), lambda i: (0, 0), pipeline_mode=pl.Buffered(1)),
    ]
    out_specs = [pl.BlockSpec((tm, w), lambda i: (i, 0)) for w in widths]
    out_shape = [jax.ShapeDtypeStruct((n, w), F32) for w in widths]
    args = [x, g, w_bf]
    if cast_src is not None:
        rows, cols = cast_src.shape
        assert rows % steps == 0
        slab = pl.BlockSpec((rows // steps, cols), lambda i: (i, 0))
        in_specs.append(slab)
        out_specs.append(slab)
        out_shape.append(jax.ShapeDtypeStruct((rows, cols), BF16))
        args.append(cast_src)
    return pl.pallas_call(
        functools.partial(_inproj_body, with_cast=cast_src is not None),
        grid=(steps,),
        in_specs=in_specs,
        out_specs=out_specs,
        out_shape=out_shape,
        compiler_params=_cparams(("arbitrary",)),
        name="inproj",
    )(*args)


def _group_mean_sq(x, gmat):
    x2 = x * x
    hi = x2.astype(BF16)
    lo = (x2 - hi.astype(F32)).astype(BF16)
    cols = []
    for c in range(x.shape[1] // MXU_DIM):
        sl = slice(c * MXU_DIM, (c + 1) * MXU_DIM)
        cols.append(jnp.dot(hi[:, sl], gmat, preferred_element_type=F32)
                    + jnp.dot(lo[:, sl], gmat, preferred_element_type=F32))
    return cols[0] if len(cols) == 1 else jnp.concatenate(cols, axis=1)


def _tile_lanes(t, reps):
    return t if reps == 1 else jnp.concatenate([t] * reps, axis=1)


def _head_norm_rope(x, gain, cos2, sin2, gmat):
    width = x.shape[1]
    y = x * lax.rsqrt(_group_mean_sq(x, gmat) + EPS) * gain
    lane = lax.broadcasted_iota(jnp.int32, y.shape, 1)
    first_half = (lane & (HEAD_DIM // 2)) == 0
    partner = jnp.where(first_half,
                        pltpu.roll(y, width - HEAD_DIM // 2, axis=1),
                        pltpu.roll(y, HEAD_DIM // 2, axis=1))
    reps = width // LANES
    return y * _tile_lanes(cos2, reps) + partner * _tile_lanes(sin2, reps)


def _rms_rows(x, gain):
    ms = jnp.mean(x * x, axis=-1, keepdims=True)
    return x * lax.rsqrt(ms + EPS) * gain


def _attn_prompt_body(q_ref, k_ref, v_ref, cos_ref, sin_ref, gq_ref, gk_ref, sink_ref, gao_ref, gmat_ref, bias_ref,
                      ao_ref, klast_ref, kprev, vprev, lg_scr, p_scr, *, nb):
    n = pl.program_id(1)
    blk = WINDOW
    rows = GQA * blk

    @pl.when(n == 0)
    def _():
        kprev[...] = jnp.zeros_like(kprev)
        vprev[...] = jnp.zeros_like(vprev)

    gmat = gmat_ref[...]
    cos2 = cos_ref[...]
    sin2 = sin_ref[...]
    q = _head_norm_rope(q_ref[...], gq_ref[...], cos2, sin2, gmat) * (HEAD_DIM ** -0.5)
    k = _head_norm_rope(k_ref[...], gk_ref[...], cos2, sin2, gmat)
    v = v_ref[...]
    kc = jnp.concatenate([kprev[...], k], axis=0).astype(BF16)
    vc = jnp.concatenate([vprev[...], v], axis=0).astype(BF16)
    qb = q.astype(BF16)
    bias = bias_ref[jnp.minimum(n, 1)]

    for g in range(N_KV_HEADS):
        qg = jnp.concatenate(
            [qb[:, (g * GQA + h) * HEAD_DIM:(g * GQA + h + 1) * HEAD_DIM] for h in range(GQA)], axis=0)
        kg = kc[:, g * HEAD_DIM:(g + 1) * HEAD_DIM]
        lg_scr[g * rows:(g + 1) * rows, :] = (
            lax.dot_general(qg, kg, (((1,), (1,)), ((), ())), preferred_element_type=F32) + bias)
    lg = lg_scr[...]
    sink = sink_ref[...]
    m = jnp.maximum(jnp.max(lg, axis=-1, keepdims=True), sink)
    e = jnp.exp(lg - jnp.concatenate([m, m], axis=1))
    den = jnp.sum(e, axis=-1, keepdims=True) + jnp.exp(sink - m)
    p_scr[...] = (e / jnp.concatenate([den, den], axis=1)).astype(BF16)
    pieces = []
    for g in range(N_KV_HEADS):
        vg = vc[:, g * HEAD_DIM:(g + 1) * HEAD_DIM]
        og = jnp.dot(p_scr[g * rows:(g + 1) * rows, :], vg, preferred_element_type=F32)
        pieces.extend(og[h * blk:(h + 1) * blk] for h in range(GQA))
    ao = jnp.concatenate(pieces, axis=1)
    ao_ref[...] = _rms_rows(ao, gao_ref[...]).astype(BF16)

    kprev[...] = k
    vprev[...] = v

    @pl.when(n == nb - 1)
    def _():
        klast_ref[0] = k


def _attn_prompt(q, k, v, cos2, sin2, gq, gk, sink_col, gao, gmat, batch, seq):
    nb = seq // WINDOW
    row = lambda b, n: (b * nb + n, 0)
    const2 = lambda b, n: (0, 0)
    const3 = lambda b, n: (0, 0, 0)
    i = np.arange(GQA * WINDOW)[:, None] % WINDOW
    j = np.arange(2 * WINDOW)[None, :]
    band = (j > i) & (j <= i + WINDOW)
    bias = np.stack([np.where(band & (j >= WINDOW), 0.0, NEG_INF), np.where(band, 0.0, NEG_INF)]).astype(np.float32)
    return pl.pallas_call(
        functools.partial(_attn_prompt_body, nb=nb),
        grid=(batch, nb),
        in_specs=[
            pl.BlockSpec((WINDOW, ATTN_WIDTH), row),
            pl.BlockSpec((WINDOW, KV_WIDTH), row),
            pl.BlockSpec((WINDOW, KV_WIDTH), row),
            pl.BlockSpec((WINDOW, LANES), lambda b, n: (n, 0)),
            pl.BlockSpec((WINDOW, LANES), lambda b, n: (n, 0)),
            pl.BlockSpec((1, ATTN_WIDTH), const2),
            pl.BlockSpec((1, KV_WIDTH), const2),
            pl.BlockSpec((N_Q_HEADS * WINDOW, LANES), const2),
            pl.BlockSpec((1, ATTN_WIDTH), const2),
            pl.BlockSpec((MXU_DIM, MXU_DIM), const2),
            pl.BlockSpec((2, GQA * WINDOW, 2 * WINDOW), const3),
        ],
        out_specs=[
            pl.BlockSpec((WINDOW, ATTN_WIDTH), row),
            pl.BlockSpec((1, WINDOW, KV_WIDTH), lambda b, n: (b, 0, 0)),
        ],
        out_shape=[
            jax.ShapeDtypeStruct((batch * seq, ATTN_WIDTH), BF16),
            jax.ShapeDtypeStruct((batch, WINDOW, KV_WIDTH), F32),
        ],
        scratch_shapes=[
            pltpu.VMEM((WINDOW, KV_WIDTH), F32),
            pltpu.VMEM((WINDOW, KV_WIDTH), F32),
            pltpu.VMEM((N_Q_HEADS * WINDOW, 2 * WINDOW), F32),
            pltpu.VMEM((N_Q_HEADS * WINDOW, 2 * WINDOW), BF16),
        ],
        compiler_params=_cparams(("arbitrary", "arbitrary")),
        name="attn_prompt",
    )(q, k, v, cos2, sin2, gq, gk, jnp.broadcast_to(sink_col.reshape(N_Q_HEADS * WINDOW, 1), (N_Q_HEADS * WINDOW, LANES)),
      gao, gmat, jnp.asarray(bias))


def _attn_sample_body(q_ref, k_ref, v_ref, ck_ref, cv_ref, cos_ref, sin_ref, gq_ref, gk_ref, sink_ref, gao_ref,
                      gmat_ref, ao_ref, ks_ref, vs_ref, *, bb, s):
    gmat = gmat_ref[...]
    cos2 = cos_ref[...]
    sin2 = sin_ref[...]
    q = _head_norm_rope(q_ref[...], gq_ref[...], cos2, sin2, gmat) * (HEAD_DIM ** -0.5)
    k = _head_norm_rope(k_ref[...], gk_ref[...], cos2, sin2, gmat)
    v = v_ref[...]
    ck = ck_ref[...]
    cv = cv_ref[...]
    k3 = k.reshape(bb, s, KV_WIDTH)
    v3 = v.reshape(bb, s, KV_WIDTH)
    ks_ref[:, :WINDOW - s, :] = ck[:, s:, :]
    ks_ref[:, WINDOW - s:, :] = k3
    vs_ref[:, :WINDOW - s, :] = cv[:, s:, :]
    vs_ref[:, WINDOW - s:, :] = v3

    q3 = q.astype(BF16).reshape(bb, s, ATTN_WIDTH)
    ckb = ck.astype(BF16)
    cvb = cv.astype(BF16)
    k3b = k3.astype(BF16)
    v3b = v3.astype(BF16)
    rows = GQA * s
    qi = lax.broadcasted_iota(jnp.int32, (bb, rows, WINDOW), 1) & (s - 1)
    tc = lax.broadcasted_iota(jnp.int32, (bb, rows, WINDOW), 2)
    mask_c = tc > qi
    qi2 = lax.broadcasted_iota(jnp.int32, (bb, rows, s), 1) & (s - 1)
    tn = lax.broadcasted_iota(jnp.int32, (bb, rows, s), 2)
    mask_n = tn <= qi2

    pieces = []
    for g in range(N_KV_HEADS):
        qg = jnp.concatenate(
            [q3[:, :, (g * GQA + h) * HEAD_DIM:(g * GQA + h + 1) * HEAD_DIM] for h in range(GQA)], axis=1)
        sl = slice(g * HEAD_DIM, (g + 1) * HEAD_DIM)
        lc = jnp.einsum("bqd,bkd->bqk", qg, ckb[:, :, sl], preferred_element_type=F32)
        ln = jnp.einsum("bqd,bkd->bqk", qg, k3b[:, :, sl], preferred_element_type=F32)
        lc = jnp.where(mask_c, lc, NEG_INF)
        ln = jnp.where(mask_n, ln, NEG_INF)
        sink = sink_ref[g]
        m = jnp.maximum(jnp.maximum(jnp.max(lc, axis=-1, keepdims=True), jnp.max(ln, axis=-1, keepdims=True)), sink)
        ec = jnp.exp(lc - m)
        en = jnp.exp(ln - m[:, :, :s])
        den = jnp.sum(ec, axis=-1, keepdims=True) + jnp.sum(en, axis=-1, keepdims=True) + jnp.exp(sink - m)
        og = (jnp.einsum("bqk,bkd->bqd", (ec / den).astype(BF16), cvb[:, :, sl], preferred_element_type=F32)
              + jnp.einsum("bqk,bkd->bqd", (en / den[:, :, :s]).astype(BF16), v3b[:, :, sl], preferred_element_type=F32))
        pieces.extend(og[:, h * s:(h + 1) * s, :] for h in range(GQA))
    ao = jnp.concatenate(pieces, axis=2).reshape(bb * s, ATTN_WIDTH)
    ao_ref[...] = _rms_rows(ao, gao_ref[...]).astype(BF16)


def _attn_sample(q, k, v, ck, cv, cos2, sin2, gq, gk, sink_col, gao, gmat, batch, s, bb):
    rows = bb * s
    row = lambda i: (i, 0)
    const2 = lambda i: (0, 0)
    b3 = lambda i: (i, 0, 0)
    return pl.pallas_call(
        functools.partial(_attn_sample_body, bb=bb, s=s),
        grid=(batch // bb,),
        in_specs=[
            pl.BlockSpec((rows, ATTN_WIDTH), row),
            pl.BlockSpec((rows, KV_WIDTH), row),
            pl.BlockSpec((rows, KV_WIDTH), row),
            pl.BlockSpec((bb, WINDOW, KV_WIDTH), b3),
            pl.BlockSpec((bb, WINDOW, KV_WIDTH), b3),
            pl.BlockSpec((rows, LANES), const2),
            pl.BlockSpec((rows, LANES), const2),
            pl.BlockSpec((1, ATTN_WIDTH), const2),
            pl.BlockSpec((1, KV_WIDTH), const2),
            pl.BlockSpec((N_KV_HEADS, GQA * s, LANES), lambda i: (0, 0, 0)),
            pl.BlockSpec((1, ATTN_WIDTH), const2),
            pl.BlockSpec((MXU_DIM, MXU_DIM), const2),
        ],
        out_specs=[
            pl.BlockSpec((rows, ATTN_WIDTH), row),
            pl.BlockSpec((bb, WINDOW, KV_WIDTH), b3),
            pl.BlockSpec((bb, WINDOW, KV_WIDTH), b3),
        ],
        out_shape=[
            jax.ShapeDtypeStruct((batch * s, ATTN_WIDTH), BF16),
            jax.ShapeDtypeStruct((batch, WINDOW, KV_WIDTH), F32),
            jax.ShapeDtypeStruct((batch, WINDOW, KV_WIDTH), F32),
        ],
        compiler_params=_cparams(("arbitrary",)),
        name="attn_sample",
    )(q, k, v, ck, cv, cos2, sin2, gq, gk, jnp.broadcast_to(sink_col, (N_KV_HEADS, GQA * s, LANES)), gao, gmat)


def _lru_gates(xc, wa_ref, wx_ref, ba, bx, lam):
    xb = xc.astype(BF16)
    r_cols, i_cols = [], []
    for c in range(LRU_WIDTH // MXU_DIM):
        sl = slice(c * MXU_DIM, (c + 1) * MXU_DIM)
        r_cols.append(jnp.dot(xb[:, sl], wa_ref[c], preferred_element_type=F32))
        i_cols.append(jnp.dot(xb[:, sl], wx_ref[c], preferred_element_type=F32))
    r = 1.0 / (1.0 + jnp.exp(-(jnp.concatenate(r_cols, axis=1) + ba)))
    ig = 1.0 / (1.0 + jnp.exp(-(jnp.concatenate(i_cols, axis=1) + bx)))
    neg_lam = -lam
    softplus = jnp.maximum(neg_lam, 0.0) + jnp.log1p(jnp.exp(-jnp.abs(neg_lam)))
    log_a = -LRU_C * r * softplus
    a = jnp.exp(log_a)
    one_minus_a2 = -jnp.tanh(log_a) * (a * a + 1.0)
    root = jnp.where(one_minus_a2 > 0.0, one_minus_a2 * lax.rsqrt(one_minus_a2), 0.0)
    u = root * (ig * xc)
    return a, u


def _segment_scan(a, u):
    rows, width = a.shape
    a = a.reshape(rows // SUBLANES, SUBLANES, width)
    u = u.reshape(rows // SUBLANES, SUBLANES, width)
    t = lax.broadcasted_iota(jnp.int32, a.shape, 1)
    d = 1
    while d < SUBLANES:
        valid = t >= d
        u = jnp.where(valid, a * pltpu.roll(u, d, axis=1) + u, u)
        a = jnp.where(valid, a * pltpu.roll(a, d, axis=1), a)
        d *= 2
    return a.reshape(rows, width), u.reshape(rows, width)


def _lru_prompt_body(xr_ref, yg_ref, cw_ref, cb_ref, wa_ref, wx_ref, ba_ref, bx_ref, lam_ref, glo_ref,
                     *refs, tb, with_transpose):
    if with_transpose:
        src_ref, lo_ref, hlast_ref, dst_ref, xpad, hcarry = refs
        dst_ref[...] = src_ref[...].T.astype(BF16)
    else:
        lo_ref, hlast_ref, xpad, hcarry = refs
    t_blk = pl.program_id(1)

    @pl.when(t_blk == 0)
    def _():
        xpad[:SUBLANES, :] = jnp.zeros((SUBLANES, LRU_WIDTH), F32)
        hcarry[...] = jnp.zeros_like(hcarry)

    xr = xr_ref[...]
    xpad[SUBLANES:, :] = xr
    xc = xr * cw_ref[CONV_W - 1:CONV_W, :] + cb_ref[...]
    for kshift in range(1, CONV_W):
        shifted = xpad[SUBLANES - kshift:SUBLANES - kshift + tb, :]
        xc = xc + shifted * cw_ref[CONV_W - 1 - kshift:CONV_W - kshift, :]
    a, u = _lru_gates(xc, wa_ref, wx_ref, ba_ref[...], bx_ref[...], lam_ref[...])
    acum, hloc = _segment_scan(a, u)
    tiles = []
    carry = hcarry[...]
    for r in range(tb // SUBLANES):
        rows = slice(r * SUBLANES, (r + 1) * SUBLANES)
        h_tile = hloc[rows] + acum[rows] * carry
        tiles.append(h_tile)
        carry = h_tile[SUBLANES - 1:SUBLANES]
    h = jnp.concatenate(tiles, axis=0)
    y = h * jax.nn.gelu(yg_ref[...])
    lo_ref[...] = _rms_rows(y, glo_ref[...]).astype(BF16)
    hlast = h[tb - 1:tb, :]
    hcarry[...] = hlast
    xpad[:SUBLANES, :] = xr[tb - SUBLANES:, :]
    hlast_ref[0] = hlast


def _lru_prompt(xr, yg, cw, cb, wa_bd, wx_bd, ba, bx, lam, glo, batch, seq, tb, transpose_src=None):
    nt = seq // tb
    row = lambda b, t: (b * nt + t, 0)
    const2 = lambda b, t: (0, 0)
    const3 = lambda b, t: (0, 0, 0)
    nchunk = LRU_WIDTH // MXU_DIM
    in_specs = [
        pl.BlockSpec((tb, LRU_WIDTH), row),
        pl.BlockSpec((tb, LRU_WIDTH), row),
        pl.BlockSpec((CONV_W, LRU_WIDTH), const2),
        pl.BlockSpec((1, LRU_WIDTH), const2),
        pl.BlockSpec((nchunk, MXU_DIM, MXU_DIM), const3),
        pl.BlockSpec((nchunk, MXU_DIM, MXU_DIM), const3),
        pl.BlockSpec((1, LRU_WIDTH), const2),
        pl.BlockSpec((1, LRU_WIDTH), const2),
        pl.BlockSpec((1, LRU_WIDTH), const2),
        pl.BlockSpec((1, LRU_WIDTH), const2),
    ]
    out_specs = [
        pl.BlockSpec((tb, LRU_WIDTH), row),
        pl.BlockSpec((1, 1, LRU_WIDTH), lambda b, t: (b, 0, 0)),
    ]
    out_shape = [
        jax.ShapeDtypeStruct((batch * seq, LRU_WIDTH), BF16),
        jax.ShapeDtypeStruct((batch, 1, LRU_WIDTH), F32),
    ]
    args = [xr, yg, cw, cb, wa_bd, wx_bd, ba, bx, lam, glo]
    if transpose_src is not None:
        rows, cols = transpose_src.shape
        assert rows % (batch * nt) == 0
        slab = rows // (batch * nt)
        in_specs.append(pl.BlockSpec((slab, cols), row))
        out_specs.append(pl.BlockSpec((cols, slab), lambda b, t: (0, b * nt + t)))
        out_shape.append(jax.ShapeDtypeStruct((cols, rows), BF16))
        args.append(transpose_src)
    return pl.pallas_call(
        functools.partial(_lru_prompt_body, tb=tb, with_transpose=transpose_src is not None),
        grid=(batch, nt),
        in_specs=in_specs,
        out_specs=out_specs,
        out_shape=out_shape,
        scratch_shapes=[pltpu.VMEM((SUBLANES + tb, LRU_WIDTH), F32), pltpu.VMEM((1, LRU_WIDTH), F32)],
        compiler_params=_cparams(("arbitrary", "arbitrary")),
        name="lru_prompt",
    )(*args)


def _lru_sample_body(xr_ref, yg_ref, cpad_ref, h0_ref, cw_ref, cb_ref, wa_ref, wx_ref, ba_ref, bx_ref, lam_ref,
                     glo_ref, lo_ref, h_ref, *, s):
    xr = xr_ref[...]
    cpad = cpad_ref[...]
    rows = xr.shape[0]
    t = lax.broadcasted_iota(jnp.int32, xr.shape, 0) & (s - 1)
    xc = xr * cw_ref[CONV_W - 1:CONV_W, :] + cb_ref[...]
    for kshift in range(1, CONV_W):
        shifted = jnp.where(t < kshift, pltpu.roll(cpad, rows - s + kshift, axis=0), pltpu.roll(xr, kshift, axis=0))
        xc = xc + shifted * cw_ref[CONV_W - 1 - kshift:CONV_W - kshift, :]
    a, u = _lru_gates(xc, wa_ref, wx_ref, ba_ref[...], bx_ref[...], lam_ref[...])
    assert s == SUBLANES
    acum, hloc = _segment_scan(a, u)
    h = hloc + acum * h0_ref[...]
    h_ref[...] = h
    y = h * jax.nn.gelu(yg_ref[...])
    lo_ref[...] = _rms_rows(y, glo_ref[...]).astype(BF16)


def _lru_sample(xr, yg, cpad, h0rows, cw, cb, wa_bd, wx_bd, ba, bx, lam, glo, s, tb):
    n = xr.shape[0]
    row = lambda i: (i, 0)
    const2 = lambda i: (0, 0)
    const3 = lambda i: (0, 0, 0)
    nchunk = LRU_WIDTH // MXU_DIM
    return pl.pallas_call(
        functools.partial(_lru_sample_body, s=s),
        grid=(n // tb,),
        in_specs=[
            pl.BlockSpec((tb, LRU_WIDTH), row),
            pl.BlockSpec((tb, LRU_WIDTH), row),
            pl.BlockSpec((tb, LRU_WIDTH), row),
            pl.BlockSpec((tb, LRU_WIDTH), row),
            pl.BlockSpec((CONV_W, LRU_WIDTH), const2),
            pl.BlockSpec((1, LRU_WIDTH), const2),
            pl.BlockSpec((nchunk, MXU_DIM, MXU_DIM), const3),
            pl.BlockSpec((nchunk, MXU_DIM, MXU_DIM), const3),
            pl.BlockSpec((1, LRU_WIDTH), const2),
            pl.BlockSpec((1, LRU_WIDTH), const2),
            pl.BlockSpec((1, LRU_WIDTH), const2),
            pl.BlockSpec((1, LRU_WIDTH), const2),
        ],
        out_specs=[pl.BlockSpec((tb, LRU_WIDTH), row), pl.BlockSpec((tb, LRU_WIDTH), row)],
        out_shape=[jax.ShapeDtypeStruct((n, LRU_WIDTH), BF16), jax.ShapeDtypeStruct((n, LRU_WIDTH), F32)],
        compiler_params=_cparams(("arbitrary",)),
        name="lru_sample",
    )(xr, yg, cpad, h0rows, cw, cb, wa_bd, wx_bd, ba, bx, lam, glo)


def _outproj_body(x_ref, ao_ref, lo_ref, w_ref, g_ref, x1_ref, hnt_ref):
    mixed = (jnp.dot(ao_ref[...], w_ref[:ATTN_WIDTH, :], preferred_element_type=F32)
             + jnp.dot(lo_ref[...], w_ref[ATTN_WIDTH:, :], preferred_element_type=F32))
    x1 = x_ref[...] + mixed
    x1_ref[...] = x1
    hnt_ref[...] = _rms_rows(x1, g_ref[...]).T.astype(BF16)


def _outproj(x, ao, lo, w_bf, g, tm):
    n = x.shape[0]
    row = lambda i: (i, 0)
    const2 = lambda i: (0, 0)
    return pl.pallas_call(
        _outproj_body,
        grid=(n // tm,),
        in_specs=[
            pl.BlockSpec((tm, D_MODEL), row),
            pl.BlockSpec((tm, ATTN_WIDTH), row),
            pl.BlockSpec((tm, LRU_WIDTH), row),
            pl.BlockSpec((D_MODEL, D_MODEL), const2),
            pl.BlockSpec((1, D_MODEL), const2),
        ],
        out_specs=[pl.BlockSpec((tm, D_MODEL), row), pl.BlockSpec((D_MODEL, tm), lambda i: (0, i))],
        out_shape=[jax.ShapeDtypeStruct((n, D_MODEL), F32), jax.ShapeDtypeStruct((D_MODEL, n), BF16)],
        compiler_params=_cparams(("arbitrary",)),
        name="outproj",
    )(x, ao, lo, w_bf, g)


PEER_CAND_COLS = tuple(min(PEER_TOPK, (PEER_TOPK + 1) // (a + 1)) for a in range(PEER_TOPK))
ROUTE_LANES = 128
ROUTE_TILES_PER_STEP = 4


def _sort_network(n):
    pairs, p = [], 1
    while p < n:
        k = p
        while k >= 1:
            for j in range(k % p, n - k, 2 * k):
                for i in range(min(k, n - j - k)):
                    if (i + j) // (2 * p) == (i + j + k) // (2 * p):
                        pairs.append((i + j, i + j + k))
            k //= 2
        p *= 2
    return tuple(pairs)


SORT16 = _sort_network(N_KEYS // SUBLANES)


def _pop_lists(heads_and_ids, nvals):
    levels = [list(lv) for lv, _ in heads_and_ids]
    ids = [i for _, i in heads_and_ids]
    big = float(SUBLANES * len(levels))
    vals = []
    for r in range(nvals):
        top = levels[0][0]
        for lv in levels[1:]:
            top = jnp.maximum(top, lv[0])
        m = jnp.max(top, axis=0, keepdims=True)
        vals.append(m)
        if r == nvals - 1:
            break
        first = None
        for lv, i in zip(levels, ids):
            cand = jnp.where(lv[0] == m, i, big)
            first = cand if first is None else jnp.minimum(first, cand)
        first = jnp.min(first, axis=0, keepdims=True)
        for lv, i in zip(levels, ids):
            hit = i == first
            live = min(len(lv), nvals - 1 - r)
            for d in range(live):
                below = lv[d + 1] if d + 1 < len(lv) else -jnp.inf
                lv[d] = jnp.where(hit, below, lv[d])
    return vals


def _top16(s, ltri, ones8):
    lanes = s.shape[1]
    lv = [s[SUBLANES * r:SUBLANES * (r + 1)] for r in range(N_KEYS // SUBLANES)]
    for i, j in SORT16:
        lv[i], lv[j] = jnp.maximum(lv[i], lv[j]), jnp.minimum(lv[i], lv[j])
    subf = lax.broadcasted_iota(jnp.int32, (SUBLANES, lanes), 0).astype(F32)
    vals = _pop_lists([(lv, subf)], PEER_TOPK)
    v16 = vals[PEER_TOPK - 1]
    gt = s > v16
    eq = s == v16
    n_gt = jnp.dot(ones8, jnp.where(gt, 1.0, 0.0).astype(BF16), preferred_element_type=F32)[0:1]
    eq_before = jnp.dot(ltri, jnp.where(eq, 1.0, 0.0).astype(BF16), preferred_element_type=F32)
    need = float(PEER_TOPK) - n_gt
    kept = jnp.where(gt, s, jnp.where(eq, jnp.where(eq_before < need, s, -jnp.inf), -jnp.inf))
    return vals, kept


def _best_sums(v0, v1):
    lanes = v0[0].shape[1]
    lo0 = jnp.concatenate(v0[:SUBLANES], axis=0)
    hi0 = jnp.concatenate(v0[SUBLANES:], axis=0)
    row8 = lax.broadcasted_iota(jnp.int32, (SUBLANES, lanes), 0)
    subf = row8.astype(F32)
    levels = []
    for b in range(PEER_TOPK):
        c = lo0 + v1[b]
        nrows = sum(1 for a in range(SUBLANES) if PEER_CAND_COLS[a] > b)
        levels.append(c if nrows == SUBLANES else jnp.where(row8 < nrows, c, -jnp.inf))
    return _pop_lists([(levels, subf), ([hi0 + v1[0]], subf + float(SUBLANES))], PEER_TOPK + 1)


def _route_body(hnt_ref, wq_ref, keys_ref, ltri_ref, ones_ref, s1m_ref, e1z_ref, t0_ref, e0_ref, s_scr, *, tm):
    qt = jnp.dot(wq_ref[...], hnt_ref[...], preferred_element_type=F32)
    half = D_KEY // 2
    for hc in range(2 * PEER_HEADS):
        qhc = qt[hc * half:(hc + 1) * half, :].astype(BF16)
        s_scr[hc] = jnp.dot(keys_ref[hc], qhc, preferred_element_type=F32)

    def route_tile(h, lanes):
        ltri = ltri_ref[...]
        ones8 = ones_ref[...]
        v0, s0m = _top16(s_scr[2 * h, :, lanes], ltri, ones8)
        v1, s1m = _top16(s_scr[2 * h + 1, :, lanes], ltri, ones8)
        best = _best_sums(v0, v1)
        mx = best[0]
        z = jnp.exp(best[0] - mx)
        for r in range(1, PEER_TOPK):
            z = z + jnp.exp(best[r] - mx)
        thr = 0.5 * (best[PEER_TOPK - 1] + best[PEER_TOPK])
        s1m_ref[h, :, lanes] = s1m
        e1z_ref[h, :, lanes] = jnp.exp(s1m - v1[0]) / z
        tiled = (N_KEYS // SUBLANES, SUBLANES, ROUTE_LANES)
        t0_ref[h, :, :, lanes] = (thr - s0m).reshape(tiled)
        e0_ref[h, :, :, lanes] = jnp.exp(s0m - v0[0]).reshape(tiled)

    per_head = tm // (ROUTE_TILES_PER_STEP * ROUTE_LANES)

    def step(it, carry):
        h = it // per_head
        c = it % per_head
        for t in range(ROUTE_TILES_PER_STEP):
            start = (c * ROUTE_TILES_PER_STEP + t) * ROUTE_LANES
            route_tile(h, pl.ds(pl.multiple_of(start, ROUTE_LANES), ROUTE_LANES))
        return carry

    lax.fori_loop(0, PEER_HEADS * per_head, step, 0)


def _route(hnt, wq_t, keys, tm):
    n = hnt.shape[1]
    blk = lambda i: (0, 0, i)
    shape = jax.ShapeDtypeStruct((PEER_HEADS, N_KEYS, n), F32)
    spec = pl.BlockSpec((PEER_HEADS, N_KEYS, tm), blk)
    shape4 = jax.ShapeDtypeStruct((PEER_HEADS, N_KEYS // SUBLANES, SUBLANES, n), F32)
    spec4 = pl.BlockSpec((PEER_HEADS, N_KEYS // SUBLANES, SUBLANES, tm), lambda i: (0, 0, 0, i))
    ltri = jnp.asarray(np.tril(np.ones((N_KEYS, N_KEYS)), -1), BF16)
    ones8 = jnp.ones((SUBLANES, N_KEYS), BF16)
    return pl.pallas_call(
        functools.partial(_route_body, tm=tm),
        grid=(n // tm,),
        in_specs=[
            pl.BlockSpec((D_MODEL, tm), lambda i: (0, i)),
            pl.BlockSpec((PEER_HEADS * D_KEY, D_MODEL), lambda i: (0, 0)),
            pl.BlockSpec((2 * PEER_HEADS, N_KEYS, D_KEY // 2), lambda i: (0, 0, 0)),
            pl.BlockSpec((N_KEYS, N_KEYS), lambda i: (0, 0)),
            pl.BlockSpec((SUBLANES, N_KEYS), lambda i: (0, 0)),
        ],
        out_specs=[spec, spec, spec4, spec4],
        out_shape=[shape, shape, shape4, shape4],
        scratch_shapes=[pltpu.VMEM((2 * PEER_HEADS, N_KEYS, tm), F32)],
        compiler_params=_cparams(("arbitrary",)),
        name="peer_route",
    )(hnt, wq_t, keys, ltri, ones8)


EXPERT_LANES = 256
EXPERT_BLOCK = SUBLANES * N_KEYS
EXPERT_ROWS = EXPERT_BLOCK // N_KEYS
U_KCHUNK = 512
U_NCHUNK = D_MODEL // U_KCHUNK
V_KCHUNK = 512
V_NCHUNK = EXPERT_BLOCK // V_KCHUNK


def _routing_weight(s1m_ref, e1z_ref, t0_ref, e0_ref, k, ii, lanes):
    width = lanes.stop - lanes.start
    trow = [jnp.broadcast_to(t0_ref[h, k, ii:ii + 1, lanes], (SUBLANES, width)) for h in range(PEER_HEADS)]
    e0row = [jnp.broadcast_to(e0_ref[h, k, ii:ii + 1, lanes], (SUBLANES, width)) for h in range(PEER_HEADS)]
    out = []
    for j in range(0, N_KEYS, SUBLANES):
        w = None
        for h in range(PEER_HEADS):
            term = jnp.where(s1m_ref[h, j:j + SUBLANES, lanes] >= trow[h], e1z_ref[h, j:j + SUBLANES, lanes], 0.0) * e0row[h]
            w = term if w is None else w + term
        out.append(w)
    return jnp.concatenate(out, axis=0)


def _experts_body(hnt_ref, *refs, tm, nk):
    u_refs = refs[:U_NCHUNK]
    vt_refs = refs[U_NCHUNK:U_NCHUNK + V_NCHUNK]
    (s1m_ref, e1z_ref, t0_ref, e0_ref, x1_ref, y_ref, acc, act, wbuf, pt_a, pt_b) = refs[U_NCHUNK + V_NCHUNK:]
    k = pl.program_id(1)
    kt = jnp.minimum(k, nk - 1)
    nlane = tm // EXPERT_LANES

    def weights():
        for ii in range(EXPERT_ROWS):
            for c in range(nlane):
                rows = slice(ii * N_KEYS, (ii + 1) * N_KEYS)
                lanes = slice(c * EXPERT_LANES, (c + 1) * EXPERT_LANES)
                wbuf[rows, lanes] = _routing_weight(s1m_ref, e1z_ref, t0_ref, e0_ref, kt, ii, lanes)

    def dot1():
        for c in range(nlane):
            lanes = slice(c * EXPERT_LANES, (c + 1) * EXPERT_LANES)
            a = None
            for kc in range(U_NCHUNK):
                d = jnp.dot(u_refs[kc][...], hnt_ref[kc * U_KCHUNK:(kc + 1) * U_KCHUNK, lanes],
                            preferred_element_type=F32)
                a = d if a is None else a + d
            act[:, lanes] = a

    def weighted_act(pt):
        for ii in range(EXPERT_ROWS):
            rows = slice(ii * N_KEYS, (ii + 1) * N_KEYS)
            for c in range(nlane):
                lanes = slice(c * EXPERT_LANES, (c + 1) * EXPERT_LANES)
                pt[rows, lanes] = (wbuf[rows, lanes] * jax.nn.gelu(act[rows, lanes])).astype(BF16)

    def dot2(pt):
        a = None
        for c in range(V_NCHUNK):
            d = jnp.dot(vt_refs[c][...], pt[c * V_KCHUNK:(c + 1) * V_KCHUNK, :], preferred_element_type=F32)
            a = d if a is None else a + d
        acc[...] += a

    @pl.when(k == 0)
    def _():
        acc[...] = jnp.zeros_like(acc)

    @pl.when(k < nk)
    def _():
        weights()
        dot1()

    @pl.when(k == 0)
    def _():
        weighted_act(pt_a)

    @pl.when((k % 2 == 0) & (k > 0) & (k < nk))
    def _():
        weighted_act(pt_a)
        dot2(pt_b)

    @pl.when(k % 2 == 1)
    def _():
        weighted_act(pt_b)
        dot2(pt_a)

    @pl.when(k == nk)
    def _():
        dot2(pt_b)
        y_ref[...] = x1_ref[...] + acc[...].T


def _experts(hnt, u_bf, vt_bf, s1m, e1z, t0, e0, x1, tm):
    n = hnt.shape[1]
    nk = N_EXPERTS // EXPERT_BLOCK
    assert nk % 2 == 0
    tok = lambda t, k: (t, 0)
    once = pl.Buffered(1)
    small = pl.BlockSpec((PEER_HEADS, N_KEYS, tm), lambda t, k: (0, 0, t), pipeline_mode=once)
    tiled = pl.BlockSpec((PEER_HEADS, N_KEYS // SUBLANES, SUBLANES, tm), lambda t, k: (0, 0, 0, t), pipeline_mode=once)
    u_specs = [pl.BlockSpec((EXPERT_BLOCK, U_KCHUNK), lambda t, k, kc=kc: (jnp.minimum(k, nk - 1), kc))
               for kc in range(U_NCHUNK)]
    vt_specs = [pl.BlockSpec((D_MODEL, V_KCHUNK), lambda t, k, c=c: (0, V_NCHUNK * jnp.maximum(k - 1, 0) + c))
                for c in range(V_NCHUNK)]
    return pl.pallas_call(
        functools.partial(_experts_body, tm=tm, nk=nk),
        grid=(n // tm, nk + 1),
        in_specs=[
            pl.BlockSpec((D_MODEL, tm), lambda t, k: (0, t)),
            *u_specs,
            *vt_specs,
            small, small, tiled, tiled,
            pl.BlockSpec((tm, D_MODEL), tok),
        ],
        out_specs=pl.BlockSpec((tm, D_MODEL), tok),
        out_shape=jax.ShapeDtypeStruct((n, D_MODEL), F32),
        scratch_shapes=[
            pltpu.VMEM((D_MODEL, tm), F32),
            pltpu.VMEM((EXPERT_BLOCK, tm), F32),
            pltpu.VMEM((EXPERT_BLOCK, tm), F32),
            pltpu.VMEM((EXPERT_BLOCK, tm), BF16),
            pltpu.VMEM((EXPERT_BLOCK, tm), BF16),
        ],
        compiler_params=_cparams(("arbitrary", "arbitrary")),
        name="peer_experts",
    )(hnt, *([u_bf] * len(u_specs)), *([vt_bf] * len(vt_specs)), s1m, e1z, t0, e0, x1)


def _rope_tables(pos):
    half = HEAD_DIM // 2
    inv = jnp.exp(-math.log(ROPE_THETA) * jnp.arange(half, dtype=F32) * (2.0 / HEAD_DIM))
    ang = pos[:, None] * inv[None, :]
    cos, sin = jnp.cos(ang), jnp.sin(ang)
    cos2 = jnp.concatenate([cos, cos, cos, cos], axis=1)
    sin2 = jnp.concatenate([-sin, sin, -sin, sin], axis=1)
    return cos2, sin2


def _block_diag(w):
    per = MXU_DIM // LRU_BLOCK
    w4 = w.reshape(LRU_WIDTH // MXU_DIM, per, LRU_BLOCK, LRU_BLOCK)
    eye = jnp.eye(per, dtype=w.dtype)
    return jnp.einsum("cpde,pq->cpdqe", w4, eye).reshape(LRU_WIDTH // MXU_DIM, MXU_DIM, MXU_DIM).astype(BF16)


def kernel(x_prompt, x_sample, cache_k, cache_v, state_conv, state_h, norm_mix_g, w_in, q_norm_g, k_norm_g, attn_sinks, conv_w, conv_b, w_rec_gate, b_rec_gate, w_in_gate, b_in_gate, lru_lambda, attn_out_g, lru_out_g, w_out, norm_ffn_g, w_peer_q, peer_sub_keys, peer_u, peer_v):
    depth = w_in.shape[0]
    assert depth == 1
    batch, seq, _ = x_prompt.shape
    dbatch, dseq, _ = x_sample.shape
    l = 0

    w_in_bf = w_in[l].astype(BF16)
    w_out_bf = w_out[l].astype(BF16)
    wq_t = w_peer_q[l].T.astype(BF16)
    keys = peer_sub_keys[l].reshape(2 * PEER_HEADS, N_KEYS, D_KEY // 2).astype(BF16)
    wa_bd = _block_diag(w_rec_gate[l])
    wx_bd = _block_diag(w_in_gate[l])
    ba = b_rec_gate[l].reshape(1, LRU_WIDTH)
    bx = b_in_gate[l].reshape(1, LRU_WIDTH)
    lam = lru_lambda[l].reshape(1, LRU_WIDTH)
    cw = conv_w[l]
    cb = conv_b[l].reshape(1, LRU_WIDTH)
    g_mix = norm_mix_g[l].reshape(1, D_MODEL)
    g_ffn = norm_ffn_g[l].reshape(1, D_MODEL)
    gq = jnp.tile(q_norm_g[l], N_Q_HEADS).reshape(1, ATTN_WIDTH)
    gk = jnp.tile(k_norm_g[l], N_KV_HEADS).reshape(1, KV_WIDTH)
    gao = attn_out_g[l].reshape(1, ATTN_WIDTH)
    glo = lru_out_g[l].reshape(1, LRU_WIDTH)
    gmat = jnp.asarray(np.kron(np.eye(MXU_DIM // HEAD_DIM), np.full((HEAD_DIM, HEAD_DIM), 1.0 / HEAD_DIM)), BF16)
    sinks = attn_sinks[l].astype(F32).reshape(N_KV_HEADS, GQA)
    sink_p = jnp.repeat(sinks, WINDOW, axis=1).reshape(N_KV_HEADS, GQA * WINDOW, 1)
    sink_s = jnp.repeat(sinks, dseq, axis=1).reshape(N_KV_HEADS, GQA * dseq, 1)
    cos_p, sin_p = _rope_tables(jnp.arange(seq, dtype=F32))
    cos_s, sin_s = _rope_tables(float(PAST_LEN) + jnp.arange(dseq, dtype=F32))

    n_p = batch * seq
    n_s = dbatch * dseq
    for n in (n_p, n_s):
        assert n % TOKENS_INPROJ == 0 and n % TOKENS_OUTPROJ == 0 and n % TOKENS_PEER == 0
    assert seq % ROWS_LRU == 0 and seq % WINDOW == 0 and n_s % ROWS_LRU == 0 and ROWS_LRU % dseq == 0
    assert dbatch % SEQS_ATTN_SAMPLE == 0 and dseq == SUBLANES and cache_k.shape[2] == WINDOW
    xp2 = x_prompt.reshape(n_p, D_MODEL)
    xs2 = x_sample.reshape(n_s, D_MODEL)

    qp, kp, vp, xrp, ygp, u_bf = _inproj(xp2, g_mix, w_in_bf, TOKENS_INPROJ, cast_src=peer_u[l])
    ao_p, k_last = _attn_prompt(qp, kp, vp, cos_p, sin_p, gq, gk, sink_p, gao, gmat, batch, seq)
    lo_p, h_p, vt_bf = _lru_prompt(xrp, ygp, cw, cb, wa_bd, wx_bd, ba, bx, lam, glo, batch, seq, ROWS_LRU,
                                   transpose_src=peer_v[l])
    x1p, hn_p = _outproj(xp2, ao_p, lo_p, w_out_bf, g_ffn, TOKENS_OUTPROJ)

    bb = SEQS_ATTN_SAMPLE
    qs, ks, vs, xrs, ygs = _inproj(xs2, g_mix, w_in_bf, TOKENS_INPROJ)
    cos_sb = jnp.tile(cos_s, (bb, 1))
    sin_sb = jnp.tile(sin_s, (bb, 1))
    ck = cache_k[l].reshape(dbatch, WINDOW, KV_WIDTH)
    cv = cache_v[l].reshape(dbatch, WINDOW, KV_WIDTH)
    ao_s, k_s, v_s = _attn_sample(qs, ks, vs, ck, cv, cos_sb, sin_sb, gq, gk, sink_s, gao, gmat, dbatch, dseq, bb)
    cpad = jnp.pad(state_conv[l], ((0, 0), (dseq - (CONV_W - 1), 0), (0, 0))).reshape(n_s, LRU_WIDTH)
    h0rows = jnp.repeat(state_h[l], dseq, axis=0)
    lo_s, h_s = _lru_sample(xrs, ygs, cpad, h0rows, cw, cb, wa_bd, wx_bd, ba, bx, lam, glo, dseq, ROWS_LRU)
    x1s, hn_s = _outproj(xs2, ao_s, lo_s, w_out_bf, g_ffn, TOKENS_OUTPROJ)

    tm = TOKENS_PEER
    outs = []
    for x1, hn in ((x1p, hn_p), (x1s, hn_s)):
        s1m, e1z, t0, e0 = _route(hn, wq_t, keys, tm)
        outs.append(_experts(hn, u_bf, vt_bf, s1m, e1z, t0, e0, x1, tm))
    y_p = outs[0].reshape(batch, seq, D_MODEL)
    y_s = outs[1].reshape(dbatch, dseq, D_MODEL)

    kv_shape = (N_KV_HEADS, HEAD_DIM)
    k_prompt = k_last.reshape(1, batch, WINDOW, *kv_shape)
    v_prompt = vp.reshape(batch, seq, *kv_shape)[:, -WINDOW:][None]
    conv_prompt = xrp.reshape(batch, seq, LRU_WIDTH)[:, -(CONV_W - 1):][None]
    h_prompt = h_p.reshape(1, batch, LRU_WIDTH)
    k_sample = k_s.reshape(1, dbatch, WINDOW, *kv_shape)
    v_sample = v_s.reshape(1, dbatch, WINDOW, *kv_shape)
    conv_sample = xrs.reshape(dbatch, dseq, LRU_WIDTH)[:, -(CONV_W - 1):][None]
    h_sample = h_s.reshape(dbatch, dseq, LRU_WIDTH)[:, -1][None]
    return (y_p, y_s, k_prompt, v_prompt, conv_prompt, h_prompt, k_sample, v_sample, conv_sample, h_sample)
```

```python
import functools
import math

import jax
import jax.numpy as jnp
import numpy as np
from jax import lax
from jax.experimental import pallas as pl
from jax.experimental.pallas import tpu as pltpu

F32 = jnp.float32
BF16 = jnp.bfloat16

D_MODEL = 2048
HEAD_DIM = 64
N_Q_HEADS = 16
N_KV_HEADS = 4
GQA = N_Q_HEADS // N_KV_HEADS
ATTN_WIDTH = N_Q_HEADS * HEAD_DIM
KV_WIDTH = N_KV_HEADS * HEAD_DIM
WINDOW = 128
ROPE_THETA = 10000.0
LRU_WIDTH = D_MODEL - ATTN_WIDTH
LRU_BLOCK = 64
CONV_W = 4
LRU_C = 8.0
N_KEYS = 128
N_EXPERTS = N_KEYS * N_KEYS
PEER_HEADS = 8
PEER_TOPK = 16
PAST_LEN = 8192
D_KEY = 256
EPS = 1e-6
NEG_INF = -1e30

MXU_DIM = 256
LANES = 128
SUBLANES = 8
VMEM_LIMIT_BYTES = 60 * 1024 * 1024

TOKENS_INPROJ = 256
TOKENS_OUTPROJ = 512
ROWS_LRU = 256
SEQS_ATTN_SAMPLE = 8
TOKENS_PEER = 512


def _cparams(sem):
    return pltpu.CompilerParams(dimension_semantics=sem, vmem_limit_bytes=VMEM_LIMIT_BYTES)


IN_CUTS = (0, ATTN_WIDTH, ATTN_WIDTH + KV_WIDTH, ATTN_WIDTH + 2 * KV_WIDTH,
           ATTN_WIDTH + 2 * KV_WIDTH + LRU_WIDTH, ATTN_WIDTH + 2 * KV_WIDTH + 2 * LRU_WIDTH)


def _inproj_body(x_ref, g_ref, w_ref, *refs, with_cast):
    if with_cast:
        src_ref, q_ref, k_ref, v_ref, xr_ref, yg_ref, dst_ref = refs
        dst_ref[...] = src_ref[...].astype(BF16)
    else:
        q_ref, k_ref, v_ref, xr_ref, yg_ref = refs
    x = x_ref[...]
    ms = jnp.mean(x * x, axis=-1, keepdims=True)
    hn = (x * lax.rsqrt(ms + EPS) * g_ref[...]).astype(BF16)
    for o_ref, lo, hi in zip((q_ref, k_ref, v_ref, xr_ref, yg_ref), IN_CUTS[:-1], IN_CUTS[1:]):
        o_ref[...] = jnp.dot(hn, w_ref[:, lo:hi], preferred_element_type=F32)


def _inproj(x, g, w_bf, tm, cast_src=None):
    n = x.shape[0]
    steps = n // tm
    widths = [hi - lo for lo, hi in zip(IN_CUTS[:-1], IN_CUTS[1:])]
    in_specs = [
        pl.BlockSpec((tm, D_MODEL), lambda i: (i, 0)),
        pl.BlockSpec((1, D_MODEL), lambda i: (0, 0)),
        pl.BlockSpec((D_MODEL, IN_CUTS[-1]), lambda i: (0, 0), pipeline_mode=pl.Buffered(1)),
    ]
    out_specs = [pl.BlockSpec((tm, w), lambda i: (i, 0)) for w in widths]
    out_shape = [jax.ShapeDtypeStruct((n, w), F32) for w in widths]
    args = [x, g, w_bf]
    if cast_src is not None:
        rows, cols = cast_src.shape
        assert rows % steps == 0
        slab = pl.BlockSpec((rows // steps, cols), lambda i: (i, 0))
        in_specs.append(slab)
        out_specs.append(slab)
        out_shape.append(jax.ShapeDtypeStruct((rows, cols), BF16))
        args.append(cast_src)
    return pl.pallas_call(
        functools.partial(_inproj_body, with_cast=cast_src is not None),
        grid=(steps,),
        in_specs=in_specs,
        out_specs=out_specs,
        out_shape=out_shape,
        compiler_params=_cparams(("arbitrary",)),
        name="inproj",
    )(*args)


def _group_mean_sq(x, gmat):
    x2 = x * x
    hi = x2.astype(BF16)
    lo = (x2 - hi.astype(F32)).astype(BF16)
    cols = []
    for c in range(x.shape[1] // MXU_DIM):
        sl = slice(c * MXU_DIM, (c + 1) * MXU_DIM)
        cols.append(jnp.dot(hi[:, sl], gmat, preferred_element_type=F32)
                    + jnp.dot(lo[:, sl], gmat, preferred_element_type=F32))
    return cols[0] if len(cols) == 1 else jnp.concatenate(cols, axis=1)


def _tile_lanes(t, reps):
    return t if reps == 1 else jnp.concatenate([t] * reps, axis=1)


def _head_norm_rope(x, gain, cos2, sin2, gmat):
    width = x.shape[1]
    y = x * lax.rsqrt(_group_mean_sq(x, gmat) + EPS) * gain
    lane = lax.broadcasted_iota(jnp.int32, y.shape, 1)
    first_half = (lane & (HEAD_DIM // 2)) == 0
    partner = jnp.where(first_half,
                        pltpu.roll(y, width - HEAD_DIM // 2, axis=1),
                        pltpu.roll(y, HEAD_DIM // 2, axis=1))
    reps = width // LANES
    return y * _tile_lanes(cos2, reps) + partner * _tile_lanes(sin2, reps)


def _rms_rows(x, gain):
    ms = jnp.mean(x * x, axis=-1, keepdims=True)
    return x * lax.rsqrt(ms + EPS) * gain


def _attn_prompt_body(q_ref, k_ref, v_ref, cos_ref, sin_ref, gq_ref, gk_ref, sink_ref, gao_ref, gmat_ref, bias_ref,
                      ao_ref, klast_ref, kprev, vprev, lg_scr, p_scr, *, nb):
    n = pl.program_id(1)
    blk = WINDOW
    rows = GQA * blk

    @pl.when(n == 0)
    def _():
        kprev[...] = jnp.zeros_like(kprev)
        vprev[...] = jnp.zeros_like(vprev)

    gmat = gmat_ref[...]
    cos2 = cos_ref[...]
    sin2 = sin_ref[...]
    q = _head_norm_rope(q_ref[...], gq_ref[...], cos2, sin2, gmat) * (HEAD_DIM ** -0.5)
    k = _head_norm_rope(k_ref[...], gk_ref[...], cos2, sin2, gmat)
    v = v_ref[...]
    kc = jnp.concatenate([kprev[...], k], axis=0).astype(BF16)
    vc = jnp.concatenate([vprev[...], v], axis=0).astype(BF16)
    qb = q.astype(BF16)
    bias = bias_ref[jnp.minimum(n, 1)]

    for g in range(N_KV_HEADS):
        qg = jnp.concatenate(
            [qb[:, (g * GQA + h) * HEAD_DIM:(g * GQA + h + 1) * HEAD_DIM] for h in range(GQA)], axis=0)
        kg = kc[:, g * HEAD_DIM:(g + 1) * HEAD_DIM]
        lg_scr[g * rows:(g + 1) * rows, :] = (
            lax.dot_general(qg, kg, (((1,), (1,)), ((), ())), preferred_element_type=F32) + bias)
    lg = lg_scr[...]
    sink = sink_ref[...]
    m = jnp.maximum(jnp.max(lg, axis=-1, keepdims=True), sink)
    e = jnp.exp(lg - jnp.concatenate([m, m], axis=1))
    den = jnp.sum(e, axis=-1, keepdims=True) + jnp.exp(sink - m)
    p_scr[...] = (e / jnp.concatenate([den, den], axis=1)).astype(BF16)
    pieces = []
    for g in range(N_KV_HEADS):
        vg = vc[:, g * HEAD_DIM:(g + 1) * HEAD_DIM]
        og = jnp.dot(p_scr[g * rows:(g + 1) * rows, :], vg, preferred_element_type=F32)
        pieces.extend(og[h * blk:(h + 1) * blk] for h in range(GQA))
    ao = jnp.concatenate(pieces, axis=1)
    ao_ref[...] = _rms_rows(ao, gao_ref[...]).astype(BF16)

    kprev[...] = k
    vprev[...] = v

    @pl.when(n == nb - 1)
    def _():
        klast_ref[0] = k


def _attn_prompt(q, k, v, cos2, sin2, gq, gk, sink_col, gao, gmat, batch, seq):
    nb = seq // WINDOW
    row = lambda b, n: (b * nb + n, 0)
    const2 = lambda b, n: (0, 0)
    const3 = lambda b, n: (0, 0, 0)
    i = np.arange(GQA * WINDOW)[:, None] % WINDOW
    j = np.arange(2 * WINDOW)[None, :]
    band = (j > i) & (j <= i + WINDOW)
    bias = np.stack([np.where(band & (j >= WINDOW), 0.0, NEG_INF), np.where(band, 0.0, NEG_INF)]).astype(np.float32)
    return pl.pallas_call(
        functools.partial(_attn_prompt_body, nb=nb),
        grid=(batch, nb),
        in_specs=[
            pl.BlockSpec((WINDOW, ATTN_WIDTH), row),
            pl.BlockSpec((WINDOW, KV_WIDTH), row),
            pl.BlockSpec((WINDOW, KV_WIDTH), row),
            pl.BlockSpec((WINDOW, LANES), lambda b, n: (n, 0)),
            pl.BlockSpec((WINDOW, LANES), lambda b, n: (n, 0)),
            pl.BlockSpec((1, ATTN_WIDTH), const2),
            pl.BlockSpec((1, KV_WIDTH), const2),
            pl.BlockSpec((N_Q_HEADS * WINDOW, LANES), const2),
            pl.BlockSpec((1, ATTN_WIDTH), const2),
            pl.BlockSpec((MXU_DIM, MXU_DIM), const2),
            pl.BlockSpec((2, GQA * WINDOW, 2 * WINDOW), const3),
        ],
        out_specs=[
            pl.BlockSpec((WINDOW, ATTN_WIDTH), row),
            pl.BlockSpec((1, WINDOW, KV_WIDTH), lambda b, n: (b, 0, 0)),
        ],
        out_shape=[
            jax.ShapeDtypeStruct((batch * seq, ATTN_WIDTH), BF16),
            jax.ShapeDtypeStruct((batch, WINDOW, KV_WIDTH), F32),
        ],
        scratch_shapes=[
            pltpu.VMEM((WINDOW, KV_WIDTH), F32),
            pltpu.VMEM((WINDOW, KV_WIDTH), F32),
            pltpu.VMEM((N_Q_HEADS * WINDOW, 2 * WINDOW), F32),
            pltpu.VMEM((N_Q_HEADS * WINDOW, 2 * WINDOW), BF16),
        ],
        compiler_params=_cparams(("arbitrary", "arbitrary")),
        name="attn_prompt",
    )(q, k, v, cos2, sin2, gq, gk, jnp.broadcast_to(sink_col.reshape(N_Q_HEADS * WINDOW, 1), (N_Q_HEADS * WINDOW, LANES)),
      gao, gmat, jnp.asarray(bias))


def _attn_sample_body(q_ref, k_ref, v_ref, ck_ref, cv_ref, cos_ref, sin_ref, gq_ref, gk_ref, sink_ref, gao_ref,
                      gmat_ref, ao_ref, ks_ref, vs_ref, *, bb, s):
    gmat = gmat_ref[...]
    cos2 = cos_ref[...]
    sin2 = sin_ref[...]
    q = _head_norm_rope(q_ref[...], gq_ref[...], cos2, sin2, gmat) * (HEAD_DIM ** -0.5)
    k = _head_norm_rope(k_ref[...], gk_ref[...], cos2, sin2, gmat)
    v = v_ref[...]
    ck = ck_ref[...]
    cv = cv_ref[...]
    k3 = k.reshape(bb, s, KV_WIDTH)
    v3 = v.reshape(bb, s, KV_WIDTH)
    ks_ref[:, :WINDOW - s, :] = ck[:, s:, :]
    ks_ref[:, WINDOW - s:, :] = k3
    vs_ref[:, :WINDOW - s, :] = cv[:, s:, :]
    vs_ref[:, WINDOW - s:, :] = v3

    q3 = q.astype(BF16).reshape(bb, s, ATTN_WIDTH)
    ckb = ck.astype(BF16)
    cvb = cv.astype(BF16)
    k3b = k3.astype(BF16)
    v3b = v3.astype(BF16)
    rows = GQA * s
    qi = lax.broadcasted_iota(jnp.int32, (bb, rows, WINDOW), 1) & (s - 1)
    tc = lax.broadcasted_iota(jnp.int32, (bb, rows, WINDOW), 2)
    mask_c = tc > qi
    qi2 = lax.broadcasted_iota(jnp.int32, (bb, rows, s), 1) & (s - 1)
    tn = lax.broadcasted_iota(jnp.int32, (bb, rows, s), 2)
    mask_n = tn <= qi2

    pieces = []
    for g in range(N_KV_HEADS):
        qg = jnp.concatenate(
            [q3[:, :, (g * GQA + h) * HEAD_DIM:(g * GQA + h + 1) * HEAD_DIM] for h in range(GQA)], axis=1)
        sl = slice(g * HEAD_DIM, (g + 1) * HEAD_DIM)
        lc = jnp.einsum("bqd,bkd->bqk", qg, ckb[:, :, sl], preferred_element_type=F32)
        ln = jnp.einsum("bqd,bkd->bqk", qg, k3b[:, :, sl], preferred_element_type=F32)
        lc = jnp.where(mask_c, lc, NEG_INF)
        ln = jnp.where(mask_n, ln, NEG_INF)
        sink = sink_ref[g]
        m = jnp.maximum(jnp.maximum(jnp.max(lc, axis=-1, keepdims=True), jnp.max(ln, axis=-1, keepdims=True)), sink)
        ec = jnp.exp(lc - m)
        en = jnp.exp(ln - m[:, :, :s])
        den = jnp.sum(ec, axis=-1, keepdims=True) + jnp.sum(en, axis=-1, keepdims=True) + jnp.exp(sink - m)
        og = (jnp.einsum("bqk,bkd->bqd", (ec / den).astype(BF16), cvb[:, :, sl], preferred_element_type=F32)
              + jnp.einsum("bqk,bkd->bqd", (en / den[:, :, :s]).astype(BF16), v3b[:, :, sl], preferred_element_type=F32))
        pieces.extend(og[:, h * s:(h + 1) * s, :] for h in range(GQA))
    ao = jnp.concatenate(pieces, axis=2).reshape(bb * s, ATTN_WIDTH)
    ao_ref[...] = _rms_rows(ao, gao_ref[...]).astype(BF16)


def _attn_sample(q, k, v, ck, cv, cos2, sin2, gq, gk, sink_col, gao, gmat, batch, s, bb):
    rows = bb * s
    row = lambda i: (i, 0)
    const2 = lambda i: (0, 0)
    b3 = lambda i: (i, 0, 0)
    return pl.pallas_call(
        functools.partial(_attn_sample_body, bb=bb, s=s),
        grid=(batch // bb,),
        in_specs=[
            pl.BlockSpec((rows, ATTN_WIDTH), row),
            pl.BlockSpec((rows, KV_WIDTH), row),
            pl.BlockSpec((rows, KV_WIDTH), row),
            pl.BlockSpec((bb, WINDOW, KV_WIDTH), b3),
            pl.BlockSpec((bb, WINDOW, KV_WIDTH), b3),
            pl.BlockSpec((rows, LANES), const2),
            pl.BlockSpec((rows, LANES), const2),
            pl.BlockSpec((1, ATTN_WIDTH), const2),
            pl.BlockSpec((1, KV_WIDTH), const2),
            pl.BlockSpec((N_KV_HEADS, GQA * s, LANES), lambda i: (0, 0, 0)),
            pl.BlockSpec((1, ATTN_WIDTH), const2),
            pl.BlockSpec((MXU_DIM, MXU_DIM), const2),
        ],
        out_specs=[
            pl.BlockSpec((rows, ATTN_WIDTH), row),
            pl.BlockSpec((bb, WINDOW, KV_WIDTH), b3),
            pl.BlockSpec((bb, WINDOW, KV_WIDTH), b3),
        ],
        out_shape=[
            jax.ShapeDtypeStruct((batch * s, ATTN_WIDTH), BF16),
            jax.ShapeDtypeStruct((batch, WINDOW, KV_WIDTH), F32),
            jax.ShapeDtypeStruct((batch, WINDOW, KV_WIDTH), F32),
        ],
        compiler_params=_cparams(("arbitrary",)),
        name="attn_sample",
    )(q, k, v, ck, cv, cos2, sin2, gq, gk, jnp.broadcast_to(sink_col, (N_KV_HEADS, GQA * s, LANES)), gao, gmat)


def _lru_gates(xc, wa_ref, wx_ref, ba, bx, lam):
    xb = xc.astype(BF16)
    r_cols, i_cols = [], []
    for c in range(LRU_WIDTH // MXU_DIM):
        sl = slice(c * MXU_DIM, (c + 1) * MXU_DIM)
        r_cols.append(jnp.dot(xb[:, sl], wa_ref[c], preferred_element_type=F32))
        i_cols.append(jnp.dot(xb[:, sl], wx_ref[c], preferred_element_type=F32))
    r = 1.0 / (1.0 + jnp.exp(-(jnp.concatenate(r_cols, axis=1) + ba)))
    ig = 1.0 / (1.0 + jnp.exp(-(jnp.concatenate(i_cols, axis=1) + bx)))
    neg_lam = -lam
    softplus = jnp.maximum(neg_lam, 0.0) + jnp.log1p(jnp.exp(-jnp.abs(neg_lam)))
    log_a = -LRU_C * r * softplus
    a = jnp.exp(log_a)
    one_minus_a2 = -jnp.tanh(log_a) * (a * a + 1.0)
    root = jnp.where(one_minus_a2 > 0.0, one_minus_a2 * lax.rsqrt(one_minus_a2), 0.0)
    u = root * (ig * xc)
    return a, u


def _segment_scan(a, u):
    rows, width = a.shape
    a = a.reshape(rows // SUBLANES, SUBLANES, width)
    u = u.reshape(rows // SUBLANES, SUBLANES, width)
    t = lax.broadcasted_iota(jnp.int32, a.shape, 1)
    d = 1
    while d < SUBLANES:
        valid = t >= d
        u = jnp.where(valid, a * pltpu.roll(u, d, axis=1) + u, u)
        a = jnp.where(valid, a * pltpu.roll(a, d, axis=1), a)
        d *= 2
    return a.reshape(rows, width), u.reshape(rows, width)


def _lru_prompt_body(xr_ref, yg_ref, cw_ref, cb_ref, wa_ref, wx_ref, ba_ref, bx_ref, lam_ref, glo_ref,
                     *refs, tb, with_transpose):
    if with_transpose:
        src_ref, lo_ref, hlast_ref, dst_ref, xpad, hcarry = refs
        dst_ref[...] = src_ref[...].T.astype(BF16)
    else:
        lo_ref, hlast_ref, xpad, hcarry = refs
    t_blk = pl.program_id(1)

    @pl.when(t_blk == 0)
    def _():
        xpad[:SUBLANES, :] = jnp.zeros((SUBLANES, LRU_WIDTH), F32)
        hcarry[...] = jnp.zeros_like(hcarry)

    xr = xr_ref[...]
    xpad[SUBLANES:, :] = xr
    xc = xr * cw_ref[CONV_W - 1:CONV_W, :] + cb_ref[...]
    for kshift in range(1, CONV_W):
        shifted = xpad[SUBLANES - kshift:SUBLANES - kshift + tb, :]
        xc = xc + shifted * cw_ref[CONV_W - 1 - kshift:CONV_W - kshift, :]
    a, u = _lru_gates(xc, wa_ref, wx_ref, ba_ref[...], bx_ref[...], lam_ref[...])
    acum, hloc = _segment_scan(a, u)
    tiles = []
    carry = hcarry[...]
    for r in range(tb // SUBLANES):
        rows = slice(r * SUBLANES, (r + 1) * SUBLANES)
        h_tile = hloc[rows] + acum[rows] * carry
        tiles.append(h_tile)
        carry = h_tile[SUBLANES - 1:SUBLANES]
    h = jnp.concatenate(tiles, axis=0)
    y = h * jax.nn.gelu(yg_ref[...])
    lo_ref[...] = _rms_rows(y, glo_ref[...]).astype(BF16)
    hlast = h[tb - 1:tb, :]
    hcarry[...] = hlast
    xpad[:SUBLANES, :] = xr[tb - SUBLANES:, :]
    hlast_ref[0] = hlast


def _lru_prompt(xr, yg, cw, cb, wa_bd, wx_bd, ba, bx, lam, glo, batch, seq, tb, transpose_src=None):
    nt = seq // tb
    row = lambda b, t: (b * nt + t, 0)
    const2 = lambda b, t: (0, 0)
    const3 = lambda b, t: (0, 0, 0)
    nchunk = LRU_WIDTH // MXU_DIM
    in_specs = [
        pl.BlockSpec((tb, LRU_WIDTH), row),
        pl.BlockSpec((tb, LRU_WIDTH), row),
        pl.BlockSpec((CONV_W, LRU_WIDTH), const2),
        pl.BlockSpec((1, LRU_WIDTH), const2),
        pl.BlockSpec((nchunk, MXU_DIM, MXU_DIM), const3),
        pl.BlockSpec((nchunk, MXU_DIM, MXU_DIM), const3),
        pl.BlockSpec((1, LRU_WIDTH), const2),
        pl.BlockSpec((1, LRU_WIDTH), const2),
        pl.BlockSpec((1, LRU_WIDTH), const2),
        pl.BlockSpec((1, LRU_WIDTH), const2),
    ]
    out_specs = [
        pl.BlockSpec((tb, LRU_WIDTH), row),
        pl.BlockSpec((1, 1, LRU_WIDTH), lambda b, t: (b, 0, 0)),
    ]
    out_shape = [
        jax.ShapeDtypeStruct((batch * seq, LRU_WIDTH), BF16),
        jax.ShapeDtypeStruct((batch, 1, LRU_WIDTH), F32),
    ]
    args = [xr, yg, cw, cb, wa_bd, wx_bd, ba, bx, lam, glo]
    if transpose_src is not None:
        rows, cols = transpose_src.shape
        assert rows % (batch * nt) == 0
        slab = rows // (batch * nt)
        in_specs.append(pl.BlockSpec((slab, cols), row))
        out_specs.append(pl.BlockSpec((cols, slab), lambda b, t: (0, b * nt + t)))
        out_shape.append(jax.ShapeDtypeStruct((cols, rows), BF16))
        args.append(transpose_src)
    return pl.pallas_call(
        functools.partial(_lru_prompt_body, tb=tb, with_transpose=transpose_src is not None),
        grid=(batch, nt),
        in_specs=in_specs,
        out_specs=out_specs,
        out_shape=out_shape,
        scratch_shapes=[pltpu.VMEM((SUBLANES + tb, LRU_WIDTH), F32), pltpu.VMEM((1, LRU_WIDTH), F32)],
        compiler_params=_cparams(("arbitrary", "arbitrary")),
        name="lru_prompt",
    )(*args)


def _lru_sample_body(xr_ref, yg_ref, cpad_ref, h0_ref, cw_ref, cb_ref, wa_ref, wx_ref, ba_ref, bx_ref, lam_ref,
                     glo_ref, lo_ref, h_ref, *, s):
    xr = xr_ref[...]
    cpad = cpad_ref[...]
    rows = xr.shape[0]
    t = lax.broadcasted_iota(jnp.int32, xr.shape, 0) & (s - 1)
    xc = xr * cw_ref[CONV_W - 1:CONV_W, :] + cb_ref[...]
    for kshift in range(1, CONV_W):
        shifted = jnp.where(t < kshift, pltpu.roll(cpad, rows - s + kshift, axis=0), pltpu.roll(xr, kshift, axis=0))
        xc = xc + shifted * cw_ref[CONV_W - 1 - kshift:CONV_W - kshift, :]
    a, u = _lru_gates(xc, wa_ref, wx_ref, ba_ref[...], bx_ref[...], lam_ref[...])
    assert s == SUBLANES
    acum, hloc = _segment_scan(a, u)
    h = hloc + acum * h0_ref[...]
    h_ref[...] = h
    y = h * jax.nn.gelu(yg_ref[...])
    lo_ref[...] = _rms_rows(y, glo_ref[...]).astype(BF16)


def _lru_sample(xr, yg, cpad, h0rows, cw, cb, wa_bd, wx_bd, ba, bx, lam, glo, s, tb):
    n = xr.shape[0]
    row = lambda i: (i, 0)
    const2 = lambda i: (0, 0)
    const3 = lambda i: (0, 0, 0)
    nchunk = LRU_WIDTH // MXU_DIM
    return pl.pallas_call(
        functools.partial(_lru_sample_body, s=s),
        grid=(n // tb,),
        in_specs=[
            pl.BlockSpec((tb, LRU_WIDTH), row),
            pl.BlockSpec((tb, LRU_WIDTH), row),
            pl.BlockSpec((tb, LRU_WIDTH), row),
            pl.BlockSpec((tb, LRU_WIDTH), row),
            pl.BlockSpec((CONV_W, LRU_WIDTH), const2),
            pl.BlockSpec((1, LRU_WIDTH), const2),
            pl.BlockSpec((nchunk, MXU_DIM, MXU_DIM), const3),
            pl.BlockSpec((nchunk, MXU_DIM, MXU_DIM), const3),
            pl.BlockSpec((1, LRU_WIDTH), const2),
            pl.BlockSpec((1, LRU_WIDTH), const2),
            pl.BlockSpec((1, LRU_WIDTH), const2),
            pl.BlockSpec((1, LRU_WIDTH), const2),
        ],
        out_specs=[pl.BlockSpec((tb, LRU_WIDTH), row), pl.BlockSpec((tb, LRU_WIDTH), row)],
        out_shape=[jax.ShapeDtypeStruct((n, LRU_WIDTH), BF16), jax.ShapeDtypeStruct((n, LRU_WIDTH), F32)],
        compiler_params=_cparams(("arbitrary",)),
        name="lru_sample",
    )(xr, yg, cpad, h0rows, cw, cb, wa_bd, wx_bd, ba, bx, lam, glo)


def _outproj_body(xa_ref, aoa_ref, loa_ref, xb_ref, aob_ref, lob_ref, w_ref, g_ref, x1_ref, hnt_ref, *, na):
    i = pl.program_id(0)

    def run(x_ref, ao_ref, lo_ref):
        mixed = (jnp.dot(ao_ref[...], w_ref[:ATTN_WIDTH, :], preferred_element_type=F32)
                 + jnp.dot(lo_ref[...], w_ref[ATTN_WIDTH:, :], preferred_element_type=F32))
        x1 = x_ref[...] + mixed
        x1_ref[...] = x1
        hnt_ref[...] = _rms_rows(x1, g_ref[...]).T.astype(BF16)

    @pl.when(i < na)
    def _():
        run(xa_ref, aoa_ref, loa_ref)

    @pl.when(i >= na)
    def _():
        run(xb_ref, aob_ref, lob_ref)


def _outproj(xa, aoa, loa, xb, aob, lob, w_bf, g, tm):
    na, nb = xa.shape[0] // tm, xb.shape[0] // tm
    n = xa.shape[0] + xb.shape[0]
    rowa = lambda i: (jnp.minimum(i, na - 1), 0)
    rowb = lambda i: (jnp.maximum(i - na, 0), 0)
    const2 = lambda i: (0, 0)
    return pl.pallas_call(
        functools.partial(_outproj_body, na=na),
        grid=(na + nb,),
        in_specs=[
            pl.BlockSpec((tm, D_MODEL), rowa),
            pl.BlockSpec((tm, ATTN_WIDTH), rowa),
            pl.BlockSpec((tm, LRU_WIDTH), rowa),
            pl.BlockSpec((tm, D_MODEL), rowb),
            pl.BlockSpec((tm, ATTN_WIDTH), rowb),
            pl.BlockSpec((tm, LRU_WIDTH), rowb),
            pl.BlockSpec((D_MODEL, D_MODEL), const2),
            pl.BlockSpec((1, D_MODEL), const2),
        ],
        out_specs=[pl.BlockSpec((tm, D_MODEL), lambda i: (i, 0)), pl.BlockSpec((D_MODEL, tm), lambda i: (0, i))],
        out_shape=[jax.ShapeDtypeStruct((n, D_MODEL), F32), jax.ShapeDtypeStruct((D_MODEL, n), BF16)],
        compiler_params=_cparams(("arbitrary",)),
        name="outproj",
    )(xa, aoa, loa, xb, aob, lob, w_bf, g)


PEER_CAND_COLS = tuple(min(PEER_TOPK, (PEER_TOPK + 1) // (a + 1)) for a in range(PEER_TOPK))
ROUTE_LANES = 128
ROUTE_TILES_PER_STEP = 4


def _sort_network(n):
    pairs, p = [], 1
    while p < n:
        k = p
        while k >= 1:
            for j in range(k % p, n - k, 2 * k):
                for i in range(min(k, n - j - k)):
                    if (i + j) // (2 * p) == (i + j + k) // (2 * p):
                        pairs.append((i + j, i + j + k))
            k //= 2
        p *= 2
    return tuple(pairs)


SORT16 = _sort_network(N_KEYS // SUBLANES)


def _pop_lists(heads_and_ids, nvals):
    levels = [list(lv) for lv, _ in heads_and_ids]
    ids = [i for _, i in heads_and_ids]
    big = float(SUBLANES * len(levels))
    vals = []
    for r in range(nvals):
        top = levels[0][0]
        for lv in levels[1:]:
            top = jnp.maximum(top, lv[0])
        m = jnp.max(top, axis=0, keepdims=True)
        vals.append(m)
        if r == nvals - 1:
            break
        first = None
        for lv, i in zip(levels, ids):
            cand = jnp.where(lv[0] == m, i, big)
            first = cand if first is None else jnp.minimum(first, cand)
        first = jnp.min(first, axis=0, keepdims=True)
        for lv, i in zip(levels, ids):
            hit = i == first
            live = min(len(lv), nvals - 1 - r)
            for d in range(live):
                below = lv[d + 1] if d + 1 < len(lv) else -jnp.inf
                lv[d] = jnp.where(hit, below, lv[d])
    return vals


def _top16(s, ltri, ones8):
    lanes = s.shape[1]
    lv = [s[SUBLANES * r:SUBLANES * (r + 1)] for r in range(N_KEYS // SUBLANES)]
    for i, j in SORT16:
        lv[i], lv[j] = jnp.maximum(lv[i], lv[j]), jnp.minimum(lv[i], lv[j])
    subf = lax.broadcasted_iota(jnp.int32, (SUBLANES, lanes), 0).astype(F32)
    vals = _pop_lists([(lv, subf)], PEER_TOPK)
    v16 = vals[PEER_TOPK - 1]
    gt = s > v16
    eq = s == v16
    n_gt = jnp.dot(ones8, jnp.where(gt, 1.0, 0.0).astype(BF16), preferred_element_type=F32)[0:1]
    eq_before = jnp.dot(ltri, jnp.where(eq, 1.0, 0.0).astype(BF16), preferred_element_type=F32)
    need = float(PEER_TOPK) - n_gt
    kept = jnp.where(gt, s, jnp.where(eq, jnp.where(eq_before < need, s, -jnp.inf), -jnp.inf))
    return vals, kept


def _best_sums(v0, v1):
    lanes = v0[0].shape[1]
    lo0 = jnp.concatenate(v0[:SUBLANES], axis=0)
    hi0 = jnp.concatenate(v0[SUBLANES:], axis=0)
    row8 = lax.broadcasted_iota(jnp.int32, (SUBLANES, lanes), 0)
    subf = row8.astype(F32)
    levels = []
    for b in range(PEER_TOPK):
        c = lo0 + v1[b]
        nrows = sum(1 for a in range(SUBLANES) if PEER_CAND_COLS[a] > b)
        levels.append(c if nrows == SUBLANES else jnp.where(row8 < nrows, c, -jnp.inf))
    return _pop_lists([(levels, subf), ([hi0 + v1[0]], subf + float(SUBLANES))], PEER_TOPK + 1)


def _route_body(hnt_ref, wq_ref, keys_ref, ltri_ref, ones_ref, s1m_ref, e1z_ref, t0_ref, e0_ref, s_scr, *, tm):
    qt = jnp.dot(wq_ref[...], hnt_ref[...], preferred_element_type=F32)
    half = D_KEY // 2
    for hc in range(2 * PEER_HEADS):
        qhc = qt[hc * half:(hc + 1) * half, :].astype(BF16)
        s_scr[hc] = jnp.dot(keys_ref[hc], qhc, preferred_element_type=F32)

    def route_tile(h, lanes):
        ltri = ltri_ref[...]
        ones8 = ones_ref[...]
        v0, s0m = _top16(s_scr[2 * h, :, lanes], ltri, ones8)
        v1, s1m = _top16(s_scr[2 * h + 1, :, lanes], ltri, ones8)
        best = _best_sums(v0, v1)
        mx = best[0]
        z = jnp.exp(best[0] - mx)
        for r in range(1, PEER_TOPK):
            z = z + jnp.exp(best[r] - mx)
        thr = 0.5 * (best[PEER_TOPK - 1] + best[PEER_TOPK])
        s1m_ref[h, :, lanes] = s1m
        e1z_ref[h, :, lanes] = jnp.exp(s1m - v1[0]) / z
        tiled = (N_KEYS // SUBLANES, SUBLANES, ROUTE_LANES)
        t0_ref[h, :, :, lanes] = (thr - s0m).reshape(tiled)
        e0_ref[h, :, :, lanes] = jnp.exp(s0m - v0[0]).reshape(tiled)

    per_head = tm // (ROUTE_TILES_PER_STEP * ROUTE_LANES)

    def step(it, carry):
        h = it // per_head
        c = it % per_head
        for t in range(ROUTE_TILES_PER_STEP):
            start = (c * ROUTE_TILES_PER_STEP + t) * ROUTE_LANES
            route_tile(h, pl.ds(pl.multiple_of(start, ROUTE_LANES), ROUTE_LANES))
        return carry

    lax.fori_loop(0, PEER_HEADS * per_head, step, 0)


def _route(hnt, wq_t, keys, tm):
    n = hnt.shape[1]
    blk = lambda i: (0, 0, i)
    shape = jax.ShapeDtypeStruct((PEER_HEADS, N_KEYS, n), F32)
    spec = pl.BlockSpec((PEER_HEADS, N_KEYS, tm), blk)
    shape4 = jax.ShapeDtypeStruct((PEER_HEADS, N_KEYS // SUBLANES, SUBLANES, n), F32)
    spec4 = pl.BlockSpec((PEER_HEADS, N_KEYS // SUBLANES, SUBLANES, tm), lambda i: (0, 0, 0, i))
    ltri = jnp.asarray(np.tril(np.ones((N_KEYS, N_KEYS)), -1), BF16)
    ones8 = jnp.ones((SUBLANES, N_KEYS), BF16)
    return pl.pallas_call(
        functools.partial(_route_body, tm=tm),
        grid=(n // tm,),
        in_specs=[
            pl.BlockSpec((D_MODEL, tm), lambda i: (0, i)),
            pl.BlockSpec((PEER_HEADS * D_KEY, D_MODEL), lambda i: (0, 0)),
            pl.BlockSpec((2 * PEER_HEADS, N_KEYS, D_KEY // 2), lambda i: (0, 0, 0)),
            pl.BlockSpec((N_KEYS, N_KEYS), lambda i: (0, 0)),
            pl.BlockSpec((SUBLANES, N_KEYS), lambda i: (0, 0)),
        ],
        out_specs=[spec, spec, spec4, spec4],
        out_shape=[shape, shape, shape4, shape4],
        scratch_shapes=[pltpu.VMEM((2 * PEER_HEADS, N_KEYS, tm), F32)],
        compiler_params=_cparams(("arbitrary",)),
        name="peer_route",
    )(hnt, wq_t, keys, ltri, ones8)


EXPERT_LANES = 256
EXPERT_BLOCK = SUBLANES * N_KEYS
EXPERT_ROWS = EXPERT_BLOCK // N_KEYS
U_KCHUNK = 512
U_NCHUNK = D_MODEL // U_KCHUNK
V_KCHUNK = 512
V_NCHUNK = EXPERT_BLOCK // V_KCHUNK


def _routing_weight(s1m_ref, e1z_ref, t0_ref, e0_ref, k, ii, lanes):
    width = lanes.stop - lanes.start
    trow = [jnp.broadcast_to(t0_ref[h, k, ii:ii + 1, lanes], (SUBLANES, width)) for h in range(PEER_HEADS)]
    e0row = [jnp.broadcast_to(e0_ref[h, k, ii:ii + 1, lanes], (SUBLANES, width)) for h in range(PEER_HEADS)]
    out = []
    for j in range(0, N_KEYS, SUBLANES):
        w = None
        for h in range(PEER_HEADS):
            term = jnp.where(s1m_ref[h, j:j + SUBLANES, lanes] >= trow[h], e1z_ref[h, j:j + SUBLANES, lanes], 0.0) * e0row[h]
            w = term if w is None else w + term
        out.append(w)
    return jnp.concatenate(out, axis=0)


def _experts_body(hnt_ref, *refs, tm, nk, ta):
    u_refs = refs[:U_NCHUNK]
    vt_refs = refs[U_NCHUNK:U_NCHUNK + V_NCHUNK]
    (s1m_ref, e1z_ref, t0_ref, e0_ref, x1_ref, ya_ref, yb_ref, acc, act, wbuf, pt_a, pt_b) = refs[U_NCHUNK + V_NCHUNK:]
    k = pl.program_id(1)
    kt = jnp.minimum(k, nk - 1)
    nlane = tm // EXPERT_LANES

    def weights():
        for ii in range(EXPERT_ROWS):
            for c in range(nlane):
                rows = slice(ii * N_KEYS, (ii + 1) * N_KEYS)
                lanes = slice(c * EXPERT_LANES, (c + 1) * EXPERT_LANES)
                wbuf[rows, lanes] = _routing_weight(s1m_ref, e1z_ref, t0_ref, e0_ref, kt, ii, lanes)

    def dot1():
        for c in range(nlane):
            lanes = slice(c * EXPERT_LANES, (c + 1) * EXPERT_LANES)
            a = None
            for kc in range(U_NCHUNK):
                d = jnp.dot(u_refs[kc][...], hnt_ref[kc * U_KCHUNK:(kc + 1) * U_KCHUNK, lanes],
                            preferred_element_type=F32)
                a = d if a is None else a + d
            act[:, lanes] = a

    def weighted_act(pt):
        for ii in range(EXPERT_ROWS):
            rows = slice(ii * N_KEYS, (ii + 1) * N_KEYS)
            for c in range(nlane):
                lanes = slice(c * EXPERT_LANES, (c + 1) * EXPERT_LANES)
                pt[rows, lanes] = (wbuf[rows, lanes] * jax.nn.gelu(act[rows, lanes])).astype(BF16)

    def dot2(pt):
        a = None
        for c in range(V_NCHUNK):
            d = jnp.dot(vt_refs[c][...], pt[c * V_KCHUNK:(c + 1) * V_KCHUNK, :], preferred_element_type=F32)
            a = d if a is None else a + d
        acc[...] += a

    @pl.when(k == 0)
    def _():
        acc[...] = jnp.zeros_like(acc)

    @pl.when(k < nk)
    def _():
        weights()
        dot1()

    @pl.when(k == 0)
    def _():
        weighted_act(pt_a)

    @pl.when((k % 2 == 0) & (k > 0) & (k < nk))
    def _():
        weighted_act(pt_a)
        dot2(pt_b)

    @pl.when(k % 2 == 1)
    def _():
        weighted_act(pt_b)
        dot2(pt_a)

    @pl.when((k == nk) & (pl.program_id(0) < ta))
    def _():
        dot2(pt_b)
        ya_ref[...] = x1_ref[...] + acc[...].T

    @pl.when((k == nk) & (pl.program_id(0) >= ta))
    def _():
        dot2(pt_b)
        yb_ref[...] = x1_ref[...] + acc[...].T


def _experts(hnt, u_bf, vt_bf, s1m, e1z, t0, e0, x1, tm, n_first):
    n = hnt.shape[1]
    ta = n_first // tm
    nk = N_EXPERTS // EXPERT_BLOCK
    assert nk % 2 == 0
    tok = lambda t, k: (t, 0)
    once = pl.Buffered(1)
    small = pl.BlockSpec((PEER_HEADS, N_KEYS, tm), lambda t, k: (0, 0, t), pipeline_mode=once)
    tiled = pl.BlockSpec((PEER_HEADS, N_KEYS // SUBLANES, SUBLANES, tm), lambda t, k: (0, 0, 0, t), pipeline_mode=once)
    u_specs = [pl.BlockSpec((EXPERT_BLOCK, U_KCHUNK), lambda t, k, kc=kc: (jnp.minimum(k, nk - 1), kc))
               for kc in range(U_NCHUNK)]
    vt_specs = [pl.BlockSpec((D_MODEL, V_KCHUNK), lambda t, k, c=c: (0, V_NCHUNK * jnp.maximum(k - 1, 0) + c))
                for c in range(V_NCHUNK)]
    return pl.pallas_call(
        functools.partial(_experts_body, tm=tm, nk=nk, ta=ta),
        grid=(n // tm, nk + 1),
        in_specs=[
            pl.BlockSpec((D_MODEL, tm), lambda t, k: (0, t), pipeline_mode=once),
            *u_specs,
            *vt_specs,
            small, small, tiled, tiled,
            pl.BlockSpec((tm, D_MODEL), tok, pipeline_mode=once),
        ],
        out_specs=[pl.BlockSpec((tm, D_MODEL), lambda t, k: (jnp.minimum(t, ta - 1), 0)),
                   pl.BlockSpec((tm, D_MODEL), lambda t, k: (jnp.maximum(t - ta, 0), 0))],
        out_shape=[jax.ShapeDtypeStruct((n_first, D_MODEL), F32), jax.ShapeDtypeStruct((n - n_first, D_MODEL), F32)],
        scratch_shapes=[
            pltpu.VMEM((D_MODEL, tm), F32),
            pltpu.VMEM((EXPERT_BLOCK, tm), F32),
            pltpu.VMEM((EXPERT_BLOCK, tm), F32),
            pltpu.VMEM((EXPERT_BLOCK, tm), BF16),
            pltpu.VMEM((EXPERT_BLOCK, tm), BF16),
        ],
        compiler_params=_cparams(("arbitrary", "arbitrary")),
        name="peer_experts",
    )(hnt, *([u_bf] * len(u_specs)), *([vt_bf] * len(vt_specs)), s1m, e1z, t0, e0, x1)


def _rope_tables(pos):
    half = HEAD_DIM // 2
    inv = jnp.exp(-math.log(ROPE_THETA) * jnp.arange(half, dtype=F32) * (2.0 / HEAD_DIM))
    ang = pos[:, None] * inv[None, :]
    cos, sin = jnp.cos(ang), jnp.sin(ang)
    cos2 = jnp.concatenate([cos, cos, cos, cos], axis=1)
    sin2 = jnp.concatenate([-sin, sin, -sin, sin], axis=1)
    return cos2, sin2


def _block_diag(w):
    per = MXU_DIM // LRU_BLOCK
    w4 = w.reshape(LRU_WIDTH // MXU_DIM, per, LRU_BLOCK, LRU_BLOCK)
    eye = jnp.eye(per, dtype=w.dtype)
    return jnp.einsum("cpde,pq->cpdqe", w4, eye).reshape(LRU_WIDTH // MXU_DIM, MXU_DIM, MXU_DIM).astype(BF16)


def kernel(x_prompt, x_sample, cache_k, cache_v, state_conv, state_h, norm_mix_g, w_in, q_norm_g, k_norm_g, attn_sinks, conv_w, conv_b, w_rec_gate, b_rec_gate, w_in_gate, b_in_gate, lru_lambda, attn_out_g, lru_out_g, w_out, norm_ffn_g, w_peer_q, peer_sub_keys, peer_u, peer_v):
    depth = w_in.shape[0]
    assert depth == 1
    batch, seq, _ = x_prompt.shape
    dbatch, dseq, _ = x_sample.shape
    l = 0

    w_in_bf = w_in[l].astype(BF16)
    w_out_bf = w_out[l].astype(BF16)
    wq_t = w_peer_q[l].T.astype(BF16)
    keys = peer_sub_keys[l].reshape(2 * PEER_HEADS, N_KEYS, D_KEY // 2).astype(BF16)
    wa_bd = _block_diag(w_rec_gate[l])
    wx_bd = _block_diag(w_in_gate[l])
    ba = b_rec_gate[l].reshape(1, LRU_WIDTH)
    bx = b_in_gate[l].reshape(1, LRU_WIDTH)
    lam = lru_lambda[l].reshape(1, LRU_WIDTH)
    cw = conv_w[l]
    cb = conv_b[l].reshape(1, LRU_WIDTH)
    g_mix = norm_mix_g[l].reshape(1, D_MODEL)
    g_ffn = norm_ffn_g[l].reshape(1, D_MODEL)
    gq = jnp.tile(q_norm_g[l], N_Q_HEADS).reshape(1, ATTN_WIDTH)
    gk = jnp.tile(k_norm_g[l], N_KV_HEADS).reshape(1, KV_WIDTH)
    gao = attn_out_g[l].reshape(1, ATTN_WIDTH)
    glo = lru_out_g[l].reshape(1, LRU_WIDTH)
    gmat = jnp.asarray(np.kron(np.eye(MXU_DIM // HEAD_DIM), np.full((HEAD_DIM, HEAD_DIM), 1.0 / HEAD_DIM)), BF16)
    sinks = attn_sinks[l].astype(F32).reshape(N_KV_HEADS, GQA)
    sink_p = jnp.repeat(sinks, WINDOW, axis=1).reshape(N_KV_HEADS, GQA * WINDOW, 1)
    sink_s = jnp.repeat(sinks, dseq, axis=1).reshape(N_KV_HEADS, GQA * dseq, 1)
    cos_p, sin_p = _rope_tables(jnp.arange(seq, dtype=F32))
    cos_s, sin_s = _rope_tables(float(PAST_LEN) + jnp.arange(dseq, dtype=F32))

    n_p = batch * seq
    n_s = dbatch * dseq
    for n in (n_p, n_s):
        assert n % TOKENS_INPROJ == 0 and n % TOKENS_OUTPROJ == 0 and n % TOKENS_PEER == 0
    assert seq % ROWS_LRU == 0 and seq % WINDOW == 0 and n_s % ROWS_LRU == 0 and ROWS_LRU % dseq == 0
    assert dbatch % SEQS_ATTN_SAMPLE == 0 and dseq == SUBLANES and cache_k.shape[2] == WINDOW
    xp2 = x_prompt.reshape(n_p, D_MODEL)
    xs2 = x_sample.reshape(n_s, D_MODEL)

    qp, kp, vp, xrp, ygp, u_bf = _inproj(xp2, g_mix, w_in_bf, TOKENS_INPROJ, cast_src=peer_u[l])
    ao_p, k_last = _attn_prompt(qp, kp, vp, cos_p, sin_p, gq, gk, sink_p, gao, gmat, batch, seq)
    lo_p, h_p, vt_bf = _lru_prompt(xrp, ygp, cw, cb, wa_bd, wx_bd, ba, bx, lam, glo, batch, seq, ROWS_LRU,
                                   transpose_src=peer_v[l])

    bb = SEQS_ATTN_SAMPLE
    qs, ks, vs, xrs, ygs = _inproj(xs2, g_mix, w_in_bf, TOKENS_INPROJ)
    cos_sb = jnp.tile(cos_s, (bb, 1))
    sin_sb = jnp.tile(sin_s, (bb, 1))
    ck = cache_k[l].reshape(dbatch, WINDOW, KV_WIDTH)
    cv = cache_v[l].reshape(dbatch, WINDOW, KV_WIDTH)
    ao_s, k_s, v_s = _attn_sample(qs, ks, vs, ck, cv, cos_sb, sin_sb, gq, gk, sink_s, gao, gmat, dbatch, dseq, bb)
    cpad = jnp.pad(state_conv[l], ((0, 0), (dseq - (CONV_W - 1), 0), (0, 0))).reshape(n_s, LRU_WIDTH)
    h0rows = jnp.repeat(state_h[l], dseq, axis=0)
    lo_s, h_s = _lru_sample(xrs, ygs, cpad, h0rows, cw, cb, wa_bd, wx_bd, ba, bx, lam, glo, dseq, ROWS_LRU)

    x1, hn = _outproj(xp2, ao_p, lo_p, xs2, ao_s, lo_s, w_out_bf, g_ffn, TOKENS_OUTPROJ)
    tm = TOKENS_PEER
    s1m, e1z, t0, e0 = _route(hn, wq_t, keys, tm)
    y_p, y_s = _experts(hn, u_bf, vt_bf, s1m, e1z, t0, e0, x1, tm, n_p)
    y_p = y_p.reshape(batch, seq, D_MODEL)
    y_s = y_s.reshape(dbatch, dseq, D_MODEL)

    kv_shape = (N_KV_HEADS, HEAD_DIM)
    k_prompt = k_last.reshape(1, batch, WINDOW, *kv_shape)
    v_prompt = vp.reshape(batch, seq, *kv_shape)[:, -WINDOW:][None]
    conv_prompt = xrp.reshape(batch, seq, LRU_WIDTH)[:, -(CONV_W - 1):][None]
    h_prompt = h_p.reshape(1, batch, LRU_WIDTH)
    k_sample = k_s.reshape(1, dbatch, WINDOW, *kv_shape)
    v_sample = v_s.reshape(1, dbatch, WINDOW, *kv_shape)
    conv_sample = xrs.reshape(dbatch, dseq, LRU_WIDTH)[:, -(CONV_W - 1):][None]
    h_sample = h_s.reshape(dbatch, dseq, LRU_WIDTH)[:, -1][None]
    return (y_p, y_s, k_prompt, v_prompt, conv_prompt, h_prompt, k_sample, v_sample, conv_sample, h_sample)
```
